```python
import math
import jax, jax.numpy as jnp
from jax import lax
import numpy as np

D_MODEL = 1024
BATCH = 4
SEQ = 8192
DEPTH = 4

EPS = 1e-6
MLA_HEADS = 4
MLA_NOPE = 128
MLA_ROPE = 64
MLA_V = 128
MLA_Q_RANK = 384
MLA_KV_RANK = 256
MLA_ROPE_BASE = 10000.0
Q_BLOCK = 128
GLA_HEADS = 4
GLA_DK = 32
GLA_DV = 64
GLA_GATE_RANK = 16
GLA_GATE_NORM = 16.0
GLA_CHUNK = 64
RET_HEADS = 4
RET_DK = 32
RET_DV = 64
RET_CHUNK = 64
RET_ROPE_BASE = 10000.0
D_MIX = MLA_HEADS * MLA_V + GLA_HEADS * GLA_DV + RET_HEADS * RET_DV
IN_SIZES = (MLA_Q_RANK, MLA_KV_RANK, MLA_ROPE,
            GLA_HEADS * GLA_DK, GLA_HEADS * GLA_DK, GLA_HEADS * GLA_DV, GLA_GATE_RANK, GLA_HEADS * GLA_DV,
            RET_HEADS * RET_DK, RET_HEADS * RET_DK, RET_HEADS * RET_DV, RET_HEADS * RET_DV)
D_IN = sum(IN_SIZES)
D_FF = 2816
CONV_W = 3

kernel_name = "hybrid_mla_gla_retnet_convffn"


def rmsnorm(x, gain):
    xf = x.astype(jnp.float32)
    y = xf * lax.rsqrt(jnp.mean(xf * xf, axis=-1, keepdims=True) + EPS)
    return (y * gain.astype(jnp.float32)).astype(x.dtype)


def head_rmsnorm(x, gain):
    H, d = x.shape[-2], x.shape[-1]
    return rmsnorm(x, gain.reshape(H, d))


def rope_tables(S, dim, base):
    inv = base ** (-(jnp.arange(0, dim, 2, dtype=jnp.float32) / dim))
    ang = jnp.arange(S, dtype=jnp.float32)[:, None] * inv[None, :]
    return jnp.cos(ang), jnp.sin(ang)


def apply_rope(x, cos, sin):
    half = x.shape[-1] // 2
    c = cos[None, :, None, :].astype(x.dtype)
    s = sin[None, :, None, :].astype(x.dtype)
    x1, x2 = x[..., :half], x[..., half:]
    return jnp.concatenate([x1 * c - x2 * s, x2 * c + x1 * s], axis=-1)


def mla_attention(q_nope, q_rope, k_nope, k_rope, v):
    B, S, H, dn = q_nope.shape
    dr = q_rope.shape[-1]
    nb = S // Q_BLOCK
    scale = (MLA_NOPE + MLA_ROPE) ** -0.5
    qn = q_nope.reshape(B, nb, Q_BLOCK, H, dn).transpose(1, 0, 2, 3, 4)
    qr = q_rope.reshape(B, nb, Q_BLOCK, H, dr).transpose(1, 0, 2, 3, 4)
    key_pos = jnp.arange(S)

    def one_block(args):
        qn_b, qr_b, start = args
        s = (jnp.einsum('bqhd,bkhd->bhqk', qn_b, k_nope)
             + jnp.einsum('bqhd,bkd->bhqk', qr_b, k_rope)).astype(jnp.float32) * scale
        q_pos = start + jnp.arange(Q_BLOCK)
        mask = key_pos[None, :] <= q_pos[:, None]
        s = jnp.where(mask[None, None], s, jnp.finfo(jnp.float32).min)
        p = jax.nn.softmax(s, axis=-1).astype(v.dtype)
        return jnp.einsum('bhqk,bkhd->bqhd', p, v)

    out = lax.map(one_block, (qn, qr, jnp.arange(nb) * Q_BLOCK))
    return out.transpose(1, 0, 2, 3, 4).reshape(B, S, H, v.shape[-1])


def mla_branch(c_q, c_kv, k_rope, q_norm, w_uq, kv_norm, w_ukv, cos, sin):
    B, S, _ = c_q.shape
    q = (rmsnorm(c_q, q_norm) @ w_uq).reshape(B, S, MLA_HEADS, MLA_NOPE + MLA_ROPE)
    q_nope, q_rope = q[..., :MLA_NOPE], apply_rope(q[..., MLA_NOPE:], cos, sin)
    kv = (rmsnorm(c_kv, kv_norm) @ w_ukv).reshape(B, S, MLA_HEADS, MLA_NOPE + MLA_V)
    k_nope, v = kv[..., :MLA_NOPE], kv[..., MLA_NOPE:]
    k_r = apply_rope(k_rope[:, :, None, :], cos, sin)[:, :, 0, :]
    return mla_attention(q_nope, q_rope, k_nope, k_r, v)


def gla_chunked(q, k, v, log_a):
    B, S, H, dk = q.shape
    dv = v.shape[-1]
    C = GLA_CHUNK
    n = S // C
    to_chunks = lambda t: t.astype(jnp.float32).reshape(B, n, C, H, t.shape[-1]).transpose(1, 0, 3, 2, 4)
    qc, kc, vc, gc = to_chunks(q), to_chunks(k), to_chunks(v), to_chunks(log_a)
    tril = jnp.tril(jnp.ones((C, C), dtype=bool))

    def step(state, inp):
        qb, kb, vb, gb = inp
        b = jnp.cumsum(gb, axis=-2)
        inter = jnp.einsum('bhik,bhkv->bhiv', qb * jnp.exp(b), state)
        diff = b[..., :, None, :] - b[..., None, :, :]
        dec = jnp.where(tril[:, :, None], jnp.exp(jnp.where(tril[:, :, None], diff, 0.0)), 0.0)
        A = jnp.einsum('bhik,bhjk,bhijk->bhij', qb, kb, dec)
        intra = jnp.einsum('bhij,bhjv->bhiv', A, vb)
        b_end = b[..., -1:, :]
        new_state = state * jnp.exp(b_end)[..., 0, :, None] + jnp.einsum(
            'bhjk,bhjv->bhkv', kb * jnp.exp(b_end - b), vb)
        return new_state, inter + intra

    state0 = jnp.zeros((B, H, dk, dv), jnp.float32)
    _, out = lax.scan(step, state0, (qc, kc, vc, gc))
    return out.transpose(1, 0, 3, 2, 4).reshape(B, S, H, dv).astype(v.dtype)


def gla_branch(q, k, v, gate_lr, g, w_gate, b_gate, out_norm):
    B, S, _ = q.shape
    qh = q.reshape(B, S, GLA_HEADS, GLA_DK) * (GLA_DK ** -0.5)
    kh = k.reshape(B, S, GLA_HEADS, GLA_DK)
    vh = v.reshape(B, S, GLA_HEADS, GLA_DV)
    gk = (gate_lr @ w_gate + b_gate).astype(jnp.float32)
    log_a = (jax.nn.log_sigmoid(gk) / GLA_GATE_NORM).reshape(B, S, GLA_HEADS, GLA_DK)
    o = gla_chunked(qh, kh, vh, log_a)
    o = head_rmsnorm(o, out_norm).reshape(B, S, GLA_HEADS * GLA_DV)
    return o * jax.nn.silu(g)


def retention_chunked(q, k, v):
    B, S, H, dk = q.shape
    dv = v.shape[-1]
    C = RET_CHUNK
    n = S // C
    to_chunks = lambda t: t.astype(jnp.float32).reshape(B, n, C, H, t.shape[-1]).transpose(0, 3, 1, 2, 4)
    qc, kc, vc = to_chunks(q), to_chunks(k), to_chunks(v)
    log_g = jnp.log(1.0 - 2.0 ** (-5.0 - jnp.arange(H, dtype=jnp.float32)))
    idx = jnp.arange(C, dtype=jnp.float32)
    diff = idx[:, None] - idx[None, :]
    D = jnp.where(diff >= 0, jnp.exp(log_g[:, None, None] * jnp.maximum(diff, 0.0)), 0.0)
    scores = jnp.einsum('bhnid,bhnjd->bhnij', qc, kc) * D[None, :, None]
    intra = jnp.einsum('bhnij,bhnjv->bhniv', scores, vc)
    k_dec = kc * jnp.exp(log_g[:, None] * (C - 1 - idx)[None, :])[None, :, None, :, None]
    U = jnp.einsum('bhnjd,bhnjv->nbhdv', k_dec, vc)
    chunk_decay = jnp.exp(log_g * C)[None, :, None, None]

    def step(R, u):
        return R * chunk_decay + u, R

    _, R_prev = lax.scan(step, jnp.zeros((B, H, dk, dv), jnp.float32), U)
    q_dec = qc * jnp.exp(log_g[:, None] * (idx + 1.0)[None, :])[None, :, None, :, None]
    inter = jnp.einsum('bhnid,nbhdv->bhniv', q_dec, R_prev)
    out = (intra + inter).transpose(0, 2, 3, 1, 4).reshape(B, S, H, dv)
    return out.astype(v.dtype)


def retention_branch(q, k, v, g, out_norm, cos, sin):
    B, S, _ = q.shape
    qh = apply_rope(q.reshape(B, S, RET_HEADS, RET_DK), cos, sin)
    kh = apply_rope(k.reshape(B, S, RET_HEADS, RET_DK), cos, sin) * (RET_DK ** -0.5)
    vh = v.reshape(B, S, RET_HEADS, RET_DV)
    o = retention_chunked(qh, kh, vh)
    o = head_rmsnorm(o, out_norm).reshape(B, S, RET_HEADS * RET_DV)
    return o * jax.nn.silu(g)


def causal_dwconv(a, w, b):
    S = a.shape[1]
    p = jnp.pad(a, ((0, 0), (CONV_W - 1, 0), (0, 0)))
    out = b
    for kk in range(CONV_W):
        out = out + p[:, kk:kk + S] * w[kk]
    return out


def setup_inputs(seed: int = 0) -> dict:
    key = jax.random.key(seed)
    ks = jax.random.split(key, 19)
    f32 = jnp.float32
    nrm = lambda k, shape, scale: jax.random.normal(k, shape, f32) * scale
    gain = lambda k, shape: 1.0 + 0.02 * jax.random.normal(k, shape, f32)
    return {
        "x": nrm(ks[0], (BATCH, SEQ, D_MODEL), 1.0),
        "attn_norm": gain(ks[1], (DEPTH, D_MODEL)),
        "w_in": nrm(ks[2], (DEPTH, D_MODEL, D_IN), D_MODEL ** -0.5),
        "mla_q_norm": gain(ks[3], (DEPTH, MLA_Q_RANK)),
        "mla_w_uq": nrm(ks[4], (DEPTH, MLA_Q_RANK, MLA_HEADS * (MLA_NOPE + MLA_ROPE)), MLA_Q_RANK ** -0.5),
        "mla_kv_norm": gain(ks[5], (DEPTH, MLA_KV_RANK)),
        "mla_w_ukv": nrm(ks[6], (DEPTH, MLA_KV_RANK, MLA_HEADS * (MLA_NOPE + MLA_V)), MLA_KV_RANK ** -0.5),
        "mla_out_norm": gain(ks[7], (DEPTH, MLA_HEADS * MLA_V)),
        "gla_w_gate": nrm(ks[8], (DEPTH, GLA_GATE_RANK, GLA_HEADS * GLA_DK), GLA_GATE_RANK ** -0.5),
        "gla_b_gate": nrm(ks[9], (DEPTH, GLA_HEADS * GLA_DK), 0.02),
        "gla_out_norm": gain(ks[10], (DEPTH, GLA_HEADS * GLA_DV)),
        "ret_out_norm": gain(ks[11], (DEPTH, RET_HEADS * RET_DV)),
        "w_out": nrm(ks[12], (DEPTH, D_MIX, D_MODEL), D_MIX ** -0.5),
        "ffn_norm": gain(ks[13], (DEPTH, D_MODEL)),
        "ffn_w_up": nrm(ks[14], (DEPTH, D_MODEL, 2 * D_FF), D_MODEL ** -0.5),
        "ffn_conv_w": nrm(ks[15], (DEPTH, CONV_W, D_FF), CONV_W ** -0.5),
        "ffn_conv_b": nrm(ks[16], (DEPTH, D_FF), 0.02),
        "ffn_w_down": nrm(ks[17], (DEPTH, D_FF, D_MODEL), D_FF ** -0.5),
        "final_norm": gain(ks[18], (D_MODEL,)),
    }


def reference(x, attn_norm, w_in, mla_q_norm, mla_w_uq, mla_kv_norm, mla_w_ukv, mla_out_norm,
              gla_w_gate, gla_b_gate, gla_out_norm, ret_out_norm, w_out, ffn_norm, ffn_w_up,
              ffn_conv_w, ffn_conv_b, ffn_w_down, final_norm):
    B, S, _ = x.shape
    mla_cos, mla_sin = rope_tables(S, MLA_ROPE, MLA_ROPE_BASE)
    ret_cos, ret_sin = rope_tables(S, RET_DK, RET_ROPE_BASE)
    split_points = []
    acc = 0
    for sz in IN_SIZES[:-1]:
        acc += sz
        split_points.append(acc)

    for l in range(DEPTH):
        h = rmsnorm(x, attn_norm[l])
        z = h @ w_in[l]
        (c_q, c_kv, k_rope, g_q, g_k, g_v, g_lr, g_g, r_q, r_k, r_v, r_g) = jnp.split(z, split_points, axis=-1)
        y_a = mla_branch(c_q, c_kv, k_rope, mla_q_norm[l], mla_w_uq[l], mla_kv_norm[l], mla_w_ukv[l],
                         mla_cos, mla_sin)
        y_a = head_rmsnorm(y_a, mla_out_norm[l]).reshape(B, S, MLA_HEADS * MLA_V)
        y_b = gla_branch(g_q, g_k, g_v, g_lr, g_g, gla_w_gate[l], gla_b_gate[l], gla_out_norm[l])
        y_c = retention_branch(r_q, r_k, r_v, r_g, ret_out_norm[l], ret_cos, ret_sin)
        x = x + jnp.concatenate([y_a, y_b, y_c], axis=-1) @ w_out[l]
        h = rmsnorm(x, ffn_norm[l])
        u = h @ ffn_w_up[l]
        a, bv = u[..., :D_FF], u[..., D_FF:]
        a = causal_dwconv(a, ffn_conv_w[l], ffn_conv_b[l])
        x = x + (jax.nn.silu(a) * bv) @ ffn_w_down[l]

    return rmsnorm(x, final_norm)
```

```python
import functools

import jax
import jax.numpy as jnp
from jax import lax
from jax.experimental import pallas as pl
from jax.experimental.pallas import tpu as pltpu

F32 = jnp.float32
BF16 = jnp.bfloat16

D_MODEL = 1024
EPS = 1e-6
MLA_HEADS, MLA_NOPE, MLA_ROPE, MLA_V = 4, 128, 64, 128
MLA_Q_RANK, MLA_KV_RANK = 384, 256
MLA_ROPE_BASE = 10000.0
MLA_QK = 256
GLA_HEADS, GLA_DK, GLA_DV = 4, 32, 64
GLA_GATE_RANK, GLA_GATE_NORM = 16, 16.0
RET_HEADS, RET_DK, RET_DV = 4, 32, 64
RET_ROPE_BASE = 10000.0
CHUNK = 64
D_FF = 2816
LANE = 128

W_MLA = 896
W_GLA = 896
W_RET = 1024
W_IN = W_MLA + W_GLA + W_RET
LIN_W = 1536

VMEM_LIMIT = 56 * 1024 * 1024


def _dot(a, b):
    return jnp.dot(a, b, preferred_element_type=F32)


def _dot_nt(a, b):
    return lax.dot_general(a, b, (((1,), (1,)), ((), ())), preferred_element_type=F32)


def _dot_tn(a, b):
    return lax.dot_general(a, b, (((0,), (0,)), ((), ())), preferred_element_type=F32)


def _rms(x, gain):
    return x * lax.rsqrt(jnp.mean(x * x, axis=-1, keepdims=True) + EPS) * gain


def _silu(x):
    return x / (1.0 + jnp.exp(-x))


def _in_proj_kernel(x_ref, g_ref, w_ref, qn_ref, wq_ref, wqr_ref, kvn_ref, wkv_ref, wg_ref, bg_ref,
                    cm_ref, sm_ref, cr_ref, sr_ref, q_out, k_out, v_out, lin_out, la_out):
    h = _rms(x_ref[...], g_ref[...]).astype(BF16)

    def proj(a, b):
        return _dot(h, w_ref[:, a:b])

    cq = _rms(proj(0, 384), qn_ref[...]).astype(BF16)
    ckv = _rms(proj(384, 640), kvn_ref[...]).astype(BF16)
    cm, sm = cm_ref[...], sm_ref[...]
    kr = (proj(640, 768) * cm + proj(768, 896) * sm).astype(BF16)
    scale = (MLA_NOPE + MLA_ROPE) ** -0.5
    for hd in range(MLA_HEADS):
        qh = _dot(cq, wq_ref[hd])
        qrot = _dot(cq, wqr_ref[hd])
        q_out[hd, :, 0:128] = (qh[:, 0:128] * scale).astype(BF16)
        q_out[hd, :, 128:256] = ((qh[:, 128:256] * cm + qrot * sm) * scale).astype(BF16)
        kvh = _dot(ckv, wkv_ref[hd])
        k_out[hd, :, 0:128] = kvh[:, 0:128].astype(BF16)
        k_out[hd, :, 128:256] = kr
        v_out[hd] = kvh[:, 128:256].astype(BF16)

    o = W_MLA
    lin_out[:, 0:128] = (proj(o, o + 128) * (GLA_DK ** -0.5)).astype(BF16)
    lin_out[:, 128:768] = proj(o + 128, o + 768).astype(BF16)
    gate = _dot(proj(o + 768, o + 896).astype(BF16), wg_ref[...]) + bg_ref[...]
    log_sig = jnp.minimum(gate, 0.0) - jnp.log(1.0 + jnp.exp(-jnp.abs(gate)))
    la_out[...] = log_sig / GLA_GATE_NORM

    o = W_MLA + W_GLA
    cr, sr = cr_ref[...], sr_ref[...]
    lin_out[:, 768:896] = (proj(o, o + 128) * cr + proj(o + 256, o + 384) * sr).astype(BF16)
    lin_out[:, 896:1024] = ((proj(o + 128, o + 256) * cr + proj(o + 384, o + 512) * sr)
                            * (RET_DK ** -0.5)).astype(BF16)
    lin_out[:, 1024:1536] = proj(o + 512, o + 1024).astype(BF16)


def _in_proj(x, gain, w, qn, wq, wqr, kvn, wkv, wg, bg, cm, sm, cr, sr, *, seq, tm):
    t = x.shape[0]
    n_pos = seq // tm
    full = lambda shape: pl.BlockSpec(shape, lambda i: (0,) * len(shape))
    tab = pl.BlockSpec((tm, LANE), lambda i: (i % n_pos, 0))
    return pl.pallas_call(
        _in_proj_kernel,
        grid=(t // tm,),
        in_specs=[
            pl.BlockSpec((tm, D_MODEL), lambda i: (i, 0)),
            full((1, D_MODEL)), full((D_MODEL, W_IN)),
            full((1, MLA_Q_RANK)), full((MLA_HEADS, MLA_Q_RANK, MLA_QK)), full((MLA_HEADS, MLA_Q_RANK, LANE)),
            full((1, MLA_KV_RANK)), full((MLA_HEADS, MLA_KV_RANK, 256)),
            full((LANE, LANE)), full((1, LANE)),
            tab, tab, tab, tab,
        ],
        out_specs=[
            pl.BlockSpec((MLA_HEADS, tm, MLA_QK), lambda i: (0, i, 0)),
            pl.BlockSpec((MLA_HEADS, tm, MLA_QK), lambda i: (0, i, 0)),
            pl.BlockSpec((MLA_HEADS, tm, MLA_V), lambda i: (0, i, 0)),
            pl.BlockSpec((tm, LIN_W), lambda i: (i, 0)),
            pl.BlockSpec((tm, LANE), lambda i: (i, 0)),
        ],
        out_shape=[
            jax.ShapeDtypeStruct((MLA_HEADS, t, MLA_QK), BF16),
            jax.ShapeDtypeStruct((MLA_HEADS, t, MLA_QK), BF16),
            jax.ShapeDtypeStruct((MLA_HEADS, t, MLA_V), BF16),
            jax.ShapeDtypeStruct((t, LIN_W), BF16),
            jax.ShapeDtypeStruct((t, LANE), F32),
        ],
        compiler_params=pltpu.CompilerParams(
            dimension_semantics=("arbitrary",), vmem_limit_bytes=VMEM_LIMIT),
        name="in_proj",
    )(x, gain, w, qn, wq, wqr, kvn, wkv, wg, bg, cm, sm, cr, sr)


def _mla_kernel(q_ref, k_ref, v_ref, g_ref, o_ref, m_ref, l_ref, acc_ref, *, tq):
    qi = pl.program_id(2)
    q = q_ref[0]
    m_ref[...] = jnp.full(m_ref.shape, -1e30, F32)
    l_ref[...] = jnp.zeros(l_ref.shape, F32)
    acc_ref[...] = jnp.zeros(acc_ref.shape, F32)

    def step(j, masked):
        start = pl.multiple_of(j * tq, tq)
        k = k_ref[0, pl.ds(start, tq), :]
        v = v_ref[0, pl.ds(start, tq), :]
        s = _dot_nt(q, k)
        if masked:
            row = lax.broadcasted_iota(jnp.int32, s.shape, 0)
            col = lax.broadcasted_iota(jnp.int32, s.shape, 1)
            s = jnp.where(col <= row, s, -1e30)
        m_old = m_ref[...]
        m_new = jnp.maximum(m_old, jnp.max(s, axis=-1, keepdims=True))
        p = jnp.exp(s - m_new)
        alpha = jnp.exp(m_old - m_new)
        l_ref[...] = alpha * l_ref[...] + jnp.sum(p, axis=-1, keepdims=True)
        acc_ref[...] = alpha * acc_ref[...] + _dot(p.astype(BF16), v)
        m_ref[...] = m_new

    def body(j, carry):
        step(j, False)
        return carry

    lax.fori_loop(0, qi, body, 0)
    step(qi, True)
    o = acc_ref[...] / l_ref[...]
    o_ref[...] = _rms(o, g_ref[0]).astype(o_ref.dtype)


def _mla(q, k, v, gain, *, batch, seq, tq):
    nq = seq // tq
    t = batch * seq
    return pl.pallas_call(
        functools.partial(_mla_kernel, tq=tq),
        grid=(MLA_HEADS, batch, nq),
        in_specs=[
            pl.BlockSpec((1, tq, MLA_QK), lambda h, b, i: (h, b * nq + i, 0)),
            pl.BlockSpec((1, seq, MLA_QK), lambda h, b, i: (h, b, 0)),
            pl.BlockSpec((1, seq, MLA_V), lambda h, b, i: (h, b, 0)),
            pl.BlockSpec((1, 1, MLA_V), lambda h, b, i: (h, 0, 0)),
        ],
        out_specs=pl.BlockSpec((tq, MLA_V), lambda h, b, i: (b * nq + i, h)),
        out_shape=jax.ShapeDtypeStruct((t, MLA_HEADS * MLA_V), BF16),
        scratch_shapes=[
            pltpu.VMEM((tq, 1), F32), pltpu.VMEM((tq, 1), F32), pltpu.VMEM((tq, MLA_V), F32)],
        compiler_params=pltpu.CompilerParams(
            dimension_semantics=("arbitrary", "arbitrary", "arbitrary"), vmem_limit_bytes=VMEM_LIMIT),
        name="mla",
    )(q, k, v, gain)


def _split3(x):
    hi = x.astype(BF16)
    r1 = x - hi.astype(F32)
    mid = r1.astype(BF16)
    lo = (r1 - mid.astype(F32)).astype(BF16)
    return hi, mid, lo


def _mm_exact(sel, x):
    hi, mid, lo = _split3(x)
    return _dot(sel, hi) + _dot(sel, mid) + _dot(sel, lo)


def _head_rows(a, n_lanes_per_head):
    lane = lax.broadcasted_iota(jnp.int32, a.shape, 1)
    shift = n_lanes_per_head.bit_length() - 1
    zero = jnp.zeros_like(a)
    return jnp.concatenate(
        [jnp.where((lane >> shift) == hd, a, zero) for hd in range(4)], axis=0)


def _seg_mean64(o2, ind):
    hi = o2.astype(BF16)
    lo = (o2 - hi.astype(F32)).astype(BF16)
    return _dot(hi, ind) + _dot(lo, ind)


def _mixers_kernel(lin_ref, la_ref, gng_ref, rng_ref, y_ref, sg_ref, sr_ref, *, nchunk):
    C = CHUNK

    @pl.when(pl.program_id(1) == 0)
    def _():
        sg_ref[...] = jnp.zeros(sg_ref.shape, F32)
        sr_ref[...] = jnp.zeros(sr_ref.shape, F32)

    def rows(c):
        return slice(c * C, (c + 1) * C)

    def lanes(c):
        return slice(c * LANE, (c + 1) * LANE)

    def wide(a):
        return jnp.concatenate([a[rows(c)] for c in range(nchunk)], axis=1)

    r_cc = lax.broadcasted_iota(jnp.int32, (C, C), 0)
    c_cc = lax.broadcasted_iota(jnp.int32, (C, C), 1)
    row_w = lax.broadcasted_iota(jnp.int32, (C, nchunk * LANE), 0)
    i_a = lax.broadcasted_iota(jnp.int32, (C, 256), 0)
    j_a = lax.broadcasted_iota(jnp.int32, (C, 256), 1) & (C - 1)
    h_a = lax.broadcasted_iota(jnp.int32, (C, 256), 1) >> 6
    sm = (lax.broadcasted_iota(jnp.int32, (256, LANE), 0) >> 6) == (
        lax.broadcasted_iota(jnp.int32, (256, LANE), 1) >> 5)
    r256 = lax.broadcasted_iota(jnp.int32, (256, 256), 0)
    c256 = lax.broadcasted_iota(jnp.int32, (256, 256), 1)
    ind = jnp.where((r256 >> 6) == (c256 >> 6), 1.0 / 64, 0.0).astype(BF16)

    q_w = wide(lin_ref[:, 0:128].astype(F32))
    k_w = wide(lin_ref[:, 128:256].astype(F32))
    g_w = wide(la_ref[...])
    tri = jnp.where(c_cc <= r_cc, 1.0, 0.0).astype(BF16)
    b_w = _mm_exact(tri, g_w)
    b_end = b_w[C - 1:C, :]

    a_cat = [jnp.zeros((C, 256), F32) for _ in range(nchunk)]

    def add_level(ql, kl, keep):
        for c in range(nchunk):
            kst = _head_rows(kl[:, lanes(c)], GLA_DK).astype(BF16)
            p = _dot_nt(ql[:, lanes(c)].astype(BF16), kst)
            a_cat[c] = a_cat[c] + jnp.where(keep, p, 0.0)

    for s in (1, 2, 4, 8, 16, 32):
        sh = s.bit_length() - 1
        sel = jnp.where(c_cc == ((r_cc >> (sh + 1)) << (sh + 1)) + (s - 1), 1.0, 0.0).astype(BF16)
        bref = _mm_exact(sel, b_w)
        odd = ((row_w >> sh) & 1) == 1
        qf = jnp.where(odd, jnp.exp(jnp.minimum(b_w - bref, 0.0)), 0.0)
        kf = jnp.where(odd, 0.0, jnp.exp(jnp.minimum(bref - b_w, 0.0)))
        add_level(q_w * qf, k_w * kf, (i_a >> (sh + 1)) == (j_a >> (sh + 1)))
    add_level(q_w, k_w, i_a == j_a)

    qd_w = (q_w * jnp.exp(b_w)).astype(BF16)
    kd_w = (k_w * jnp.exp(b_end - b_w)).astype(BF16)
    dec_w = jnp.exp(b_end)
    state = sg_ref[...]
    outs = []
    for c in range(nchunk):
        vc = lin_ref[rows(c), 256:512]
        intra = _dot(a_cat[c].astype(BF16), _head_rows(vc, GLA_DV))
        inter = _dot_nt(qd_w[:, lanes(c)], state.astype(BF16))
        upd = _dot_tn(vc, kd_w[:, lanes(c)])
        state = state * dec_w[:, lanes(c)] + jnp.where(sm, upd, 0.0)
        outs.append(inter + intra)
    sg_ref[...] = state
    o = jnp.concatenate(outs, axis=0)
    o = o * lax.rsqrt(_seg_mean64(o * o, ind) + EPS) * gng_ref[...]
    y_ref[:, 0:256] = (o * _silu(lin_ref[:, 512:768].astype(F32))).astype(y_ref.dtype)

    def log_gamma(hd):
        v = jnp.where(hd == 0, 2.0 ** -5, jnp.where(hd == 1, 2.0 ** -6, jnp.where(hd == 2, 2.0 ** -7, 2.0 ** -8)))
        return jnp.log(1.0 - v)

    lg_a = log_gamma(h_a)
    diff = (i_a - j_a).astype(F32)
    d_cat = jnp.where(diff >= 0, jnp.exp(lg_a * jnp.maximum(diff, 0.0)), 0.0)
    row_c = lax.broadcasted_iota(jnp.int32, (C, LANE), 0).astype(F32)
    lg_c = log_gamma(lax.broadcasted_iota(jnp.int32, (C, LANE), 1) >> 5)
    k_dec = jnp.exp(lg_c * (C - 1.0 - row_c))
    q_dec = jnp.exp(lg_c * (row_c + 1.0))
    c_dec = jnp.exp(lg_c[0:1, :] * float(C))
    state = sr_ref[...]
    outs = []
    for c in range(nchunk):
        qc = lin_ref[rows(c), 768:896]
        kc = lin_ref[rows(c), 896:1024]
        vc = lin_ref[rows(c), 1024:1280]
        p = _dot_nt(qc, _head_rows(kc, RET_DK)) * d_cat
        intra = _dot(p.astype(BF16), _head_rows(vc, RET_DV))
        inter = _dot_nt((qc.astype(F32) * q_dec).astype(BF16), state.astype(BF16))
        upd = _dot_tn(vc, (kc.astype(F32) * k_dec).astype(BF16))
        state = state * c_dec + jnp.where(sm, upd, 0.0)
        outs.append(inter + intra)
    sr_ref[...] = state
    o = jnp.concatenate(outs, axis=0)
    o = o * lax.rsqrt(_seg_mean64(o * o, ind) + EPS) * rng_ref[...]
    y_ref[:, 256:512] = (o * _silu(lin_ref[:, 1280:1536].astype(F32))).astype(y_ref.dtype)


def _mixers(lin, la, gla_gain, ret_gain, *, batch, seq, tc):
    n = seq // tc
    t = batch * seq
    return pl.pallas_call(
        functools.partial(_mixers_kernel, nchunk=tc // CHUNK),
        grid=(batch, n),
        in_specs=[
            pl.BlockSpec((tc, LIN_W), lambda b, c: (b * n + c, 0)),
            pl.BlockSpec((tc, LANE), lambda b, c: (b * n + c, 0)),
            pl.BlockSpec((1, 256), lambda b, c: (0, 0)),
            pl.BlockSpec((1, 256), lambda b, c: (0, 0)),
        ],
        out_specs=pl.BlockSpec((tc, 512), lambda b, c: (b * n + c, 0)),
        out_shape=jax.ShapeDtypeStruct((t, 512), BF16),
        scratch_shapes=[pltpu.VMEM((256, LANE), F32), pltpu.VMEM((256, LANE), F32)],
        compiler_params=pltpu.CompilerParams(
            dimension_semantics=("arbitrary", "arbitrary"), vmem_limit_bytes=VMEM_LIMIT),
        name="mixers",
    )(lin, la, gla_gain, ret_gain)


def _out_ffn_kernel(x_ref, ya_ref, ybc_ref, wo_ref, fg_ref, wa_ref, wb_ref, cw_ref, cb_ref, wd_ref, ng_ref,
                    o_ref, xn_ref, hn_ref, acc_ref, abuf_ref, tail_ref, *, tm, n_pos, final):
    i, f = pl.program_id(0), pl.program_id(1)
    nf = pl.num_programs(1)

    @pl.when(f == 0)
    def _():
        xn = x_ref[...] + _dot(ya_ref[...], wo_ref[0:512, :]) + _dot(ybc_ref[...], wo_ref[512:1024, :])
        xn_ref[...] = xn
        hn_ref[...] = _rms(xn, fg_ref[...]).astype(BF16)
        acc_ref[...] = jnp.zeros(acc_ref.shape, F32)

    hn = hn_ref[...]
    a = _dot(hn, wa_ref[...])
    bv = _dot(hn, wb_ref[...])
    prev = tail_ref[f]
    abuf_ref[0:8, :] = jnp.where(i % n_pos == 0, jnp.zeros_like(prev), prev)
    abuf_ref[8:tm + 8, :] = a
    tail_ref[f] = a[tm - 8:tm, :]
    cw = cw_ref[...]
    conv = cb_ref[...] + cw[0:1, :] * abuf_ref[6:tm + 6, :] + cw[1:2, :] * abuf_ref[7:tm + 7, :] + cw[2:3, :] * a
    act = (_silu(conv) * bv).astype(BF16)
    acc_ref[...] += _dot(act, wd_ref[...])

    @pl.when(f == nf - 1)
    def _():
        y = xn_ref[...] + acc_ref[...]
        if final:
            y = _rms(y, ng_ref[...])
        o_ref[...] = y


def _out_ffn(x, ya, ybc, wo, fgain, wup, cw, cb, wd, ngain, *, seq, tm, tf, final):
    t = x.shape[0]
    nf = D_FF // tf
    return pl.pallas_call(
        functools.partial(_out_ffn_kernel, tm=tm, n_pos=seq // tm, final=final),
        grid=(t // tm, nf),
        in_specs=[
            pl.BlockSpec((tm, D_MODEL), lambda i, f: (i, 0)),
            pl.BlockSpec((tm, 512), lambda i, f: (i, 0)),
            pl.BlockSpec((tm, 512), lambda i, f: (i, 0)),
            pl.BlockSpec((D_MODEL, D_MODEL), lambda i, f: (0, 0)),
            pl.BlockSpec((1, D_MODEL), lambda i, f: (0, 0)),
            pl.BlockSpec((D_MODEL, tf), lambda i, f: (0, f)),
            pl.BlockSpec((D_MODEL, tf), lambda i, f: (0, nf + f)),
            pl.BlockSpec((3, tf), lambda i, f: (0, f)),
            pl.BlockSpec((1, tf), lambda i, f: (0, f)),
            pl.BlockSpec((tf, D_MODEL), lambda i, f: (f, 0)),
            pl.BlockSpec((1, D_MODEL), lambda i, f: (0, 0)),
        ],
        out_specs=pl.BlockSpec((tm, D_MODEL), lambda i, f: (i, 0)),
        out_shape=jax.ShapeDtypeStruct((t, D_MODEL), F32),
        scratch_shapes=[
            pltpu.VMEM((tm, D_MODEL), F32), pltpu.VMEM((tm, D_MODEL), BF16), pltpu.VMEM((tm, D_MODEL), F32),
            pltpu.VMEM((tm + 8, tf), F32), pltpu.VMEM((nf, 8, tf), F32)],
        compiler_params=pltpu.CompilerParams(
            dimension_semantics=("arbitrary", "arbitrary"), vmem_limit_bytes=VMEM_LIMIT),
        name="out_ffn",
    )(x, ya, ybc, wo, fgain, wup, wup, cw, cb, wd, ngain)


def _rot_cols(w, heads, dim):
    w4 = w.reshape(w.shape[0], heads, dim)
    half = dim // 2
    return jnp.concatenate([-w4[..., half:], w4[..., :half]], axis=-1).reshape(w.shape)


def _rope_tables(seq, dim, base):
    inv = base ** (-(jnp.arange(0, dim, 2, dtype=F32) / dim))
    ang = jnp.arange(seq, dtype=F32)[:, None] * inv[None, :]
    return jnp.cos(ang), jnp.sin(ang)


def _pack_in_weights(w_in, w_uq, w_ukv, w_gate, b_gate):
    d = w_in.shape[0]
    z = lambda n: jnp.zeros((d, n), w_in.dtype)
    cols = {}
    off = 0
    for name, sz in (("cq", 384), ("ckv", 256), ("kr", 64), ("gq", 128), ("gk", 128), ("gv", 256), ("glr", 16),
                     ("gg", 256), ("rq", 128), ("rk", 128), ("rv", 256), ("rg", 256)):
        cols[name] = w_in[:, off:off + sz]
        off += sz
    w = jnp.concatenate([
        cols["cq"], cols["ckv"], cols["kr"], z(64), _rot_cols(cols["kr"], 1, MLA_ROPE), z(64),
        cols["gq"], cols["gk"], cols["gv"], cols["gg"], cols["glr"], z(112),
        cols["rq"], cols["rk"], _rot_cols(cols["rq"], RET_HEADS, RET_DK), _rot_cols(cols["rk"], RET_HEADS, RET_DK),
        cols["rv"], cols["rg"]], axis=1).astype(BF16)
    uq = w_uq.reshape(MLA_Q_RANK, MLA_HEADS, MLA_NOPE + MLA_ROPE).transpose(1, 0, 2)
    zq = jnp.zeros((MLA_HEADS, MLA_Q_RANK, 64), w_uq.dtype)
    wq = jnp.concatenate([uq, zq], axis=-1).astype(BF16)
    rope = uq[..., MLA_NOPE:]
    wqr = jnp.concatenate([-rope[..., 32:], rope[..., :32], zq], axis=-1).astype(BF16)
    wkv = w_ukv.reshape(MLA_KV_RANK, MLA_HEADS, MLA_NOPE + MLA_V).transpose(1, 0, 2).astype(BF16)
    wg = jnp.zeros((LANE, LANE), w_gate.dtype).at[:GLA_GATE_RANK].set(w_gate).astype(BF16)
    return w, wq, wqr, wkv, wg, b_gate.reshape(1, LANE)


def kernel(x, attn_norm, w_in, mla_q_norm, mla_w_uq, mla_kv_norm, mla_w_ukv, mla_out_norm, gla_w_gate, gla_b_gate,
           gla_out_norm, ret_out_norm, w_out, ffn_norm, ffn_w_up, ffn_conv_w, ffn_conv_b, ffn_w_down, final_norm):
    batch, seq, d = x.shape
    depth = w_in.shape[0]
    t = batch * seq
    tm_in, tq, tc, tm_ffn, tf = 512, 512, 512, 1024, 256

    mcos, msin = _rope_tables(seq, MLA_ROPE, MLA_ROPE_BASE)
    zpad = jnp.zeros((seq, 64), F32)
    cm = jnp.concatenate([mcos, mcos, zpad], axis=1)
    sm = jnp.concatenate([msin, msin, zpad], axis=1)
    rcos, rsin = _rope_tables(seq, RET_DK, RET_ROPE_BASE)
    cr = jnp.tile(jnp.concatenate([rcos, rcos], axis=1), (1, RET_HEADS))
    sr = jnp.tile(jnp.concatenate([rsin, rsin], axis=1), (1, RET_HEADS))

    xf = x.reshape(t, d)
    for l in range(depth):
        w, wq, wqr, wkv, wg, bg = _pack_in_weights(w_in[l], mla_w_uq[l], mla_w_ukv[l], gla_w_gate[l], gla_b_gate[l])
        q, k, v, lin, la = _in_proj(
            xf, attn_norm[l].reshape(1, d), w, mla_q_norm[l].reshape(1, -1), wq, wqr,
            mla_kv_norm[l].reshape(1, -1), wkv, wg, bg, cm, sm, cr, sr, seq=seq, tm=tm_in)
        ya = _mla(q, k, v, mla_out_norm[l].reshape(MLA_HEADS, 1, MLA_V), batch=batch, seq=seq, tq=tq)
        ybc = _mixers(lin, la, gla_out_norm[l].reshape(1, -1), ret_out_norm[l].reshape(1, -1),
                      batch=batch, seq=seq, tc=tc)
        xf = _out_ffn(
            xf, ya, ybc, w_out[l].astype(BF16), ffn_norm[l].reshape(1, d), ffn_w_up[l].astype(BF16),
            ffn_conv_w[l], ffn_conv_b[l].reshape(1, -1), ffn_w_down[l].astype(BF16), final_norm.reshape(1, d),
            seq=seq, tm=tm_ffn, tf=tf, final=(l == depth - 1))
    return xf.reshape(batch, seq, d)
```

```python
import functools

import jax
import jax.numpy as jnp
from jax import lax
from jax.experimental import pallas as pl
from jax.experimental.pallas import tpu as pltpu

F32 = jnp.float32
BF16 = jnp.bfloat16

D_MODEL = 1024
EPS = 1e-6
MLA_HEADS, MLA_NOPE, MLA_ROPE, MLA_V = 4, 128, 64, 128
MLA_Q_RANK, MLA_KV_RANK = 384, 256
MLA_ROPE_BASE = 10000.0
MLA_QK = 256
GLA_HEADS, GLA_DK, GLA_DV = 4, 32, 64
GLA_GATE_RANK, GLA_GATE_NORM = 16, 16.0
RET_HEADS, RET_DK, RET_DV = 4, 32, 64
RET_ROPE_BASE = 10000.0
CHUNK = 64
D_FF = 2816
LANE = 128
LOG2E = 1.4426950408889634

W_MLA = 896
W_GLA = 896
W_RET = 1024
W_IN = W_MLA + W_GLA + W_RET
LIN_W = 1536

VMEM_LIMIT = 56 * 1024 * 1024


def _dot(a, b):
    return jnp.dot(a, b, preferred_element_type=F32)


def _dot_nt(a, b):
    return lax.dot_general(a, b, (((1,), (1,)), ((), ())), preferred_element_type=F32)


def _dot_tn(a, b):
    return lax.dot_general(a, b, (((0,), (0,)), ((), ())), preferred_element_type=F32)


def _rms(x, gain):
    return x * lax.rsqrt(jnp.mean(x * x, axis=-1, keepdims=True) + EPS) * gain


def _silu(x):
    return x / (1.0 + jnp.exp(-x))


def _in_proj_kernel(x_ref, g_ref, w_ref, qn_ref, wq_ref, wqr_ref, kvn_ref, wkv_ref, wg_ref, bg_ref,
                    cm_ref, sm_ref, cr_ref, sr_ref, q_out, k_out, v_out, lin_out, la_out):
    h = _rms(x_ref[...], g_ref[...]).astype(BF16)

    def proj(a, b):
        return _dot(h, w_ref[:, a:b])

    cq = _rms(proj(0, 384), qn_ref[...]).astype(BF16)
    ckv = _rms(proj(384, 640), kvn_ref[...]).astype(BF16)
    cm, sm = cm_ref[...], sm_ref[...]
    kr = (proj(640, 768) * cm + proj(768, 896) * sm).astype(BF16)
    scale = (MLA_NOPE + MLA_ROPE) ** -0.5 * LOG2E
    ones = jnp.ones((x_ref.shape[0], MLA_V), BF16)
    for hd in range(MLA_HEADS):
        qh = _dot(cq, wq_ref[hd])
        qrot = _dot(cq, wqr_ref[hd])
        q_out[hd, :, 0:128] = (qh[:, 0:128] * scale).astype(BF16)
        q_out[hd, :, 128:256] = ((qh[:, 128:256] * cm + qrot * sm) * scale).astype(BF16)
        kvh = _dot(ckv, wkv_ref[hd])
        k_out[hd, :, 0:128] = kvh[:, 0:128].astype(BF16)
        k_out[hd, :, 128:256] = kr
        v_out[hd, :, 0:128] = kvh[:, 128:256].astype(BF16)
        v_out[hd, :, 128:256] = ones

    o = W_MLA
    lin_out[:, 0:128] = (proj(o, o + 128) * (GLA_DK ** -0.5)).astype(BF16)
    lin_out[:, 128:768] = proj(o + 128, o + 768).astype(BF16)
    gate = _dot(proj(o + 768, o + 896).astype(BF16), wg_ref[...]) + bg_ref[...]
    log_sig = jnp.minimum(gate, 0.0) - jnp.log(1.0 + jnp.exp(-jnp.abs(gate)))
    la_out[...] = log_sig / GLA_GATE_NORM

    o = W_MLA + W_GLA
    cr, sr = cr_ref[...], sr_ref[...]
    lin_out[:, 768:896] = (proj(o, o + 128) * cr + proj(o + 256, o + 384) * sr).astype(BF16)
    lin_out[:, 896:1024] = ((proj(o + 128, o + 256) * cr + proj(o + 384, o + 512) * sr)
                            * (RET_DK ** -0.5)).astype(BF16)
    lin_out[:, 1024:1536] = proj(o + 512, o + 1024).astype(BF16)


def _in_proj(x, gain, w, qn, wq, wqr, kvn, wkv, wg, bg, cm, sm, cr, sr, *, seq, tm):
    t = x.shape[0]
    n_pos = seq // tm
    full = lambda shape: pl.BlockSpec(shape, lambda i: (0,) * len(shape))
    tab = pl.BlockSpec((tm, LANE), lambda i: (i % n_pos, 0))
    return pl.pallas_call(
        _in_proj_kernel,
        grid=(t // tm,),
        in_specs=[
            pl.BlockSpec((tm, D_MODEL), lambda i: (i, 0)),
            full((1, D_MODEL)), full((D_MODEL, W_IN)),
            full((1, MLA_Q_RANK)), full((MLA_HEADS, MLA_Q_RANK, MLA_QK)), full((MLA_HEADS, MLA_Q_RANK, LANE)),
            full((1, MLA_KV_RANK)), full((MLA_HEADS, MLA_KV_RANK, 256)),
            full((LANE, LANE)), full((1, LANE)),
            tab, tab, tab, tab,
        ],
        out_specs=[
            pl.BlockSpec((MLA_HEADS, tm, MLA_QK), lambda i: (0, i, 0)),
            pl.BlockSpec((MLA_HEADS, tm, MLA_QK), lambda i: (0, i, 0)),
            pl.BlockSpec((MLA_HEADS, tm, 2 * MLA_V), lambda i: (0, i, 0)),
            pl.BlockSpec((tm, LIN_W), lambda i: (i, 0)),
            pl.BlockSpec((tm, LANE), lambda i: (i, 0)),
        ],
        out_shape=[
            jax.ShapeDtypeStruct((MLA_HEADS, t, MLA_QK), BF16),
            jax.ShapeDtypeStruct((MLA_HEADS, t, MLA_QK), BF16),
            jax.ShapeDtypeStruct((MLA_HEADS, t, 2 * MLA_V), BF16),
            jax.ShapeDtypeStruct((t, LIN_W), BF16),
            jax.ShapeDtypeStruct((t, LANE), F32),
        ],
        compiler_params=pltpu.CompilerParams(
            dimension_semantics=("arbitrary",), vmem_limit_bytes=VMEM_LIMIT),
        name="in_proj",
    )(x, gain, w, qn, wq, wqr, kvn, wkv, wg, bg, cm, sm, cr, sr)


def _mla_kernel(q_ref, k_ref, v_ref, g_ref, o_ref, m_ref, acc_ref, s_ref, *, tq, tk):
    qi = pl.program_id(2)
    m_ref[...] = jnp.full(m_ref.shape, -1e30, F32)
    acc_ref[...] = jnp.zeros(acc_ref.shape, F32)

    def scores(j, r0, nr, buf):
        start = pl.multiple_of(j * tk, tk)
        s_ref[buf, r0:r0 + nr, :] = _dot_nt(q_ref[0, r0:r0 + nr, :], k_ref[0, pl.ds(start, tk), :])

    def update(j, r0, nr, buf, diag_col0=None):
        start = pl.multiple_of(j * tk, tk)
        s = s_ref[buf, r0:r0 + nr, :]
        if diag_col0 is not None:
            row = lax.broadcasted_iota(jnp.int32, s.shape, 0) + r0
            col = lax.broadcasted_iota(jnp.int32, s.shape, 1) + diag_col0
            s = jnp.where(col <= row, s, -1e30)
        m_old = m_ref[r0:r0 + nr, :]
        m_new = jnp.maximum(m_old, jnp.max(s, axis=-1, keepdims=True))
        alpha = jnp.exp2(m_old - m_new)
        p = jnp.exp2(s - jnp.concatenate([m_new] * (tk // LANE), axis=1))
        acc_ref[r0:r0 + nr, :] = (jnp.concatenate([alpha, alpha], axis=1) * acc_ref[r0:r0 + nr, :]
                                  + _dot(p.astype(BF16), v_ref[0, pl.ds(start, tk), :]))
        m_ref[r0:r0 + nr, :] = m_new

    scores(0, 0, tq, 0)

    def body(jj, carry):
        j = 2 * jj
        scores(j + 1, 0, tq, 1)
        update(j, 0, tq, 0)
        scores(j + 2, 0, tq, 0)
        update(j + 1, 0, tq, 1)
        return carry

    lax.fori_loop(0, qi, body, 0)
    scores(2 * qi + 1, tk, tk, 1)
    update(2 * qi, 0, tq, 0, diag_col0=0)
    update(2 * qi + 1, tk, tk, 1, diag_col0=tk)
    acc = acc_ref[...]
    o = acc[:, 0:MLA_V] / acc[:, MLA_V:2 * MLA_V]
    o_ref[...] = _rms(o, g_ref[0]).astype(o_ref.dtype)


def _mla(q, k, v, gain, *, batch, seq, tq, tk):
    nq = seq // tq
    t = batch * seq
    return pl.pallas_call(
        functools.partial(_mla_kernel, tq=tq, tk=tk),
        grid=(MLA_HEADS, batch, nq),
        in_specs=[
            pl.BlockSpec((1, tq, MLA_QK), lambda h, b, i: (h, b * nq + i, 0)),
            pl.BlockSpec((1, seq, MLA_QK), lambda h, b, i: (h, b, 0)),
            pl.BlockSpec((1, seq, 2 * MLA_V), lambda h, b, i: (h, b, 0)),
            pl.BlockSpec((1, 1, MLA_V), lambda h, b, i: (h, 0, 0)),
        ],
        out_specs=pl.BlockSpec((tq, MLA_V), lambda h, b, i: (b * nq + i, h)),
        out_shape=jax.ShapeDtypeStruct((t, MLA_HEADS * MLA_V), BF16),
        scratch_shapes=[pltpu.VMEM((tq, LANE), F32), pltpu.VMEM((tq, 2 * MLA_V), F32),
                        pltpu.VMEM((2, tq, tk), F32)],
        compiler_params=pltpu.CompilerParams(
            dimension_semantics=("arbitrary", "arbitrary", "arbitrary"), vmem_limit_bytes=VMEM_LIMIT),
        name="mla",
    )(q, k, v, gain)


def _split3(x):
    hi = x.astype(BF16)
    r1 = x - hi.astype(F32)
    mid = r1.astype(BF16)
    lo = (r1 - mid.astype(F32)).astype(BF16)
    return hi, mid, lo


def _mm_exact(sel, x):
    hi, mid, lo = _split3(x)
    return _dot(sel, hi) + _dot(sel, mid) + _dot(sel, lo)


def _head_rows(a, n_lanes_per_head):
    lane = lax.broadcasted_iota(jnp.int32, a.shape, 1)
    shift = n_lanes_per_head.bit_length() - 1
    zero = jnp.zeros_like(a)
    return jnp.concatenate(
        [jnp.where((lane >> shift) == hd, a, zero) for hd in range(4)], axis=0)


def _seg_mean64(o2, ind):
    hi = o2.astype(BF16)
    lo = (o2 - hi.astype(F32)).astype(BF16)
    return _dot(hi, ind) + _dot(lo, ind)


def _mixers_kernel(lin_ref, la_ref, gng_ref, rng_ref, y_ref, sg_ref, sr_ref, *, nchunk):
    C = CHUNK

    @pl.when(pl.program_id(1) == 0)
    def _():
        sg_ref[...] = jnp.zeros(sg_ref.shape, F32)
        sr_ref[...] = jnp.zeros(sr_ref.shape, F32)

    def rows(c):
        return slice(c * C, (c + 1) * C)

    def lanes(c):
        return slice(c * LANE, (c + 1) * LANE)

    def wide(a):
        return jnp.concatenate([a[rows(c)] for c in range(nchunk)], axis=1)

    r_cc = lax.broadcasted_iota(jnp.int32, (C, C), 0)
    c_cc = lax.broadcasted_iota(jnp.int32, (C, C), 1)
    row_w = lax.broadcasted_iota(jnp.int32, (C, nchunk * LANE), 0)
    i_a = lax.broadcasted_iota(jnp.int32, (C, 256), 0)
    j_a = lax.broadcasted_iota(jnp.int32, (C, 256), 1) & (C - 1)
    h_a = lax.broadcasted_iota(jnp.int32, (C, 256), 1) >> 6
    sm = (lax.broadcasted_iota(jnp.int32, (256, LANE), 0) >> 6) == (
        lax.broadcasted_iota(jnp.int32, (256, LANE), 1) >> 5)
    r256 = lax.broadcasted_iota(jnp.int32, (256, 256), 0)
    c256 = lax.broadcasted_iota(jnp.int32, (256, 256), 1)
    ind = jnp.where((r256 >> 6) == (c256 >> 6), 1.0 / 64, 0.0).astype(BF16)

    q_w = wide(lin_ref[:, 0:128].astype(F32))
    k_w = wide(lin_ref[:, 128:256].astype(F32))
    g_w = wide(la_ref[...])
    tri = jnp.where(c_cc <= r_cc, 1.0, 0.0).astype(BF16)
    b_w = _mm_exact(tri, g_w)
    b_end = b_w[C - 1:C, :]

    a_cat = [jnp.zeros((C, 256), F32) for _ in range(nchunk)]

    def add_level(ql, kl, keep):
        for c in range(nchunk):
            kst = _head_rows(kl[:, lanes(c)], GLA_DK).astype(BF16)
            p = _dot_nt(ql[:, lanes(c)].astype(BF16), kst)
            a_cat[c] = a_cat[c] + jnp.where(keep, p, 0.0)

    for s in (1, 2, 4, 8, 16, 32):
        sh = s.bit_length() - 1
        sel = jnp.where(c_cc == ((r_cc >> (sh + 1)) << (sh + 1)) + (s - 1), 1.0, 0.0).astype(BF16)
        bref = _mm_exact(sel, b_w)
        odd = ((row_w >> sh) & 1) == 1
        qf = jnp.where(odd, jnp.exp(jnp.minimum(b_w - bref, 0.0)), 0.0)
        kf = jnp.where(odd, 0.0, jnp.exp(jnp.minimum(bref - b_w, 0.0)))
        add_level(q_w * qf, k_w * kf, (i_a >> (sh + 1)) == (j_a >> (sh + 1)))
    add_level(q_w, k_w, i_a == j_a)

    qd_w = (q_w * jnp.exp(b_w)).astype(BF16)
    kd_w = (k_w * jnp.exp(b_end - b_w)).astype(BF16)
    dec_w = jnp.exp(b_end)
    state = sg_ref[...]
    outs = []
    for c in range(nchunk):
        vc = lin_ref[rows(c), 256:512]
        intra = _dot(a_cat[c].astype(BF16), _head_rows(vc, GLA_DV))
        inter = _dot_nt(qd_w[:, lanes(c)], state.astype(BF16))
        upd = _dot_tn(vc, kd_w[:, lanes(c)])
        state = state * dec_w[:, lanes(c)] + jnp.where(sm, upd, 0.0)
        outs.append(inter + intra)
    sg_ref[...] = state
    o = jnp.concatenate(outs, axis=0)
    o = o * lax.rsqrt(_seg_mean64(o * o, ind) + EPS) * gng_ref[...]
    y_ref[:, 0:256] = (o * _silu(lin_ref[:, 512:768].astype(F32))).astype(y_ref.dtype)

    def log_gamma(hd):
        v = jnp.where(hd == 0, 2.0 ** -5, jnp.where(hd == 1, 2.0 ** -6, jnp.where(hd == 2, 2.0 ** -7, 2.0 ** -8)))
        return jnp.log(1.0 - v)

    lg_a = log_gamma(h_a)
    diff = (i_a - j_a).astype(F32)
    d_cat = jnp.where(diff >= 0, jnp.exp(lg_a * jnp.maximum(diff, 0.0)), 0.0)
    row_c = lax.broadcasted_iota(jnp.int32, (C, LANE), 0).astype(F32)
    lg_c = log_gamma(lax.broadcasted_iota(jnp.int32, (C, LANE), 1) >> 5)
    k_dec = jnp.exp(lg_c * (C - 1.0 - row_c))
    q_dec = jnp.exp(lg_c * (row_c + 1.0))
    c_dec = jnp.exp(lg_c[0:1, :] * float(C))
    state = sr_ref[...]
    outs = []
    for c in range(nchunk):
        qc = lin_ref[rows(c), 768:896]
        kc = lin_ref[rows(c), 896:1024]
        vc = lin_ref[rows(c), 1024:1280]
        p = _dot_nt(qc, _head_rows(kc, RET_DK)) * d_cat
        intra = _dot(p.astype(BF16), _head_rows(vc, RET_DV))
        inter = _dot_nt((qc.astype(F32) * q_dec).astype(BF16), state.astype(BF16))
        upd = _dot_tn(vc, (kc.astype(F32) * k_dec).astype(BF16))
        state = state * c_dec + jnp.where(sm, upd, 0.0)
        outs.append(inter + intra)
    sr_ref[...] = state
    o = jnp.concatenate(outs, axis=0)
    o = o * lax.rsqrt(_seg_mean64(o * o, ind) + EPS) * rng_ref[...]
    y_ref[:, 256:512] = (o * _silu(lin_ref[:, 1280:1536].astype(F32))).astype(y_ref.dtype)


def _mixers(lin, la, gla_gain, ret_gain, *, batch, seq, tc):
    n = seq // tc
    t = batch * seq
    return pl.pallas_call(
        functools.partial(_mixers_kernel, nchunk=tc // CHUNK),
        grid=(batch, n),
        in_specs=[
            pl.BlockSpec((tc, LIN_W), lambda b, c: (b * n + c, 0)),
            pl.BlockSpec((tc, LANE), lambda b, c: (b * n + c, 0)),
            pl.BlockSpec((1, 256), lambda b, c: (0, 0)),
            pl.BlockSpec((1, 256), lambda b, c: (0, 0)),
        ],
        out_specs=pl.BlockSpec((tc, 512), lambda b, c: (b * n + c, 0)),
        out_shape=jax.ShapeDtypeStruct((t, 512), BF16),
        scratch_shapes=[pltpu.VMEM((256, LANE), F32), pltpu.VMEM((256, LANE), F32)],
        compiler_params=pltpu.CompilerParams(
            dimension_semantics=("arbitrary", "arbitrary"), vmem_limit_bytes=VMEM_LIMIT),
        name="mixers",
    )(lin, la, gla_gain, ret_gain)


def _out_ffn_kernel(x_ref, ya_ref, ybc_ref, wo_ref, fg_ref, wup_ref, cw_ref, cb_ref, wd_ref, ng_ref,
                    o_ref, abuf_ref, *, tm, tf, n_pos, final):
    i = pl.program_id(0)
    xn = x_ref[...] + _dot(ya_ref[...], wo_ref[0:512, :]) + _dot(ybc_ref[...], wo_ref[512:1024, :])
    hn = _rms(xn, fg_ref[...]).astype(BF16)

    @pl.when(i % n_pos == 0)
    def _():
        abuf_ref[0:8, :] = jnp.zeros((8, D_FF), F32)

    @pl.when(i % n_pos != 0)
    def _():
        abuf_ref[0:8, :] = abuf_ref[tm:tm + 8, :]

    acc = None
    for f in range(D_FF // tf):
        c0, c1 = f * tf, (f + 1) * tf
        a = _dot(hn, wup_ref[:, c0:c1])
        bv = _dot(hn, wup_ref[:, D_FF + c0:D_FF + c1])
        abuf_ref[8:tm + 8, c0:c1] = a
        cw = cw_ref[:, c0:c1]
        conv = (cb_ref[:, c0:c1] + cw[0:1, :] * abuf_ref[6:tm + 6, c0:c1]
                + cw[1:2, :] * abuf_ref[7:tm + 7, c0:c1] + cw[2:3, :] * a)
        act = (_silu(conv) * bv).astype(BF16)
        down = _dot(act, wd_ref[c0:c1, :])
        acc = down if acc is None else acc + down
    acc = acc + xn
    if final:
        acc = _rms(acc, ng_ref[...])
    o_ref[...] = acc


def _out_ffn(x, ya, ybc, wo, fgain, wup, cw, cb, wd, ngain, *, seq, tm, tf, final):
    t = x.shape[0]
    once = lambda shape: pl.BlockSpec(shape, lambda i: (0,) * len(shape), pipeline_mode=pl.Buffered(1))
    return pl.pallas_call(
        functools.partial(_out_ffn_kernel, tm=tm, tf=tf, n_pos=seq // tm, final=final),
        grid=(t // tm,),
        in_specs=[
            pl.BlockSpec((tm, D_MODEL), lambda i: (i, 0)),
            pl.BlockSpec((tm, 512), lambda i: (i, 0)),
            pl.BlockSpec((tm, 512), lambda i: (i, 0)),
            once((D_MODEL, D_MODEL)), once((1, D_MODEL)), once((D_MODEL, 2 * D_FF)),
            once((3, D_FF)), once((1, D_FF)), once((D_FF, D_MODEL)), once((1, D_MODEL)),
        ],
        out_specs=pl.BlockSpec((tm, D_MODEL), lambda i: (i, 0)),
        out_shape=jax.ShapeDtypeStruct((t, D_MODEL), F32),
        scratch_shapes=[pltpu.VMEM((tm + 8, D_FF), F32)],
        compiler_params=pltpu.CompilerParams(dimension_semantics=("arbitrary",), vmem_limit_bytes=VMEM_LIMIT),
        name="out_ffn",
    )(x, ya, ybc, wo, fgain, wup, cw, cb, wd, ngain)


def _rot_cols(w, heads, dim):
    w4 = w.reshape(w.shape[0], heads, dim)
    half = dim // 2
    return jnp.concatenate([-w4[..., half:], w4[..., :half]], axis=-1).reshape(w.shape)


def _rope_tables(seq, dim, base):
    inv = base ** (-(jnp.arange(0, dim, 2, dtype=F32) / dim))
    ang = jnp.arange(seq, dtype=F32)[:, None] * inv[None, :]
    return jnp.cos(ang), jnp.sin(ang)


def _pack_in_weights(w_in, w_uq, w_ukv, w_gate, b_gate):
    d = w_in.shape[0]
    z = lambda n: jnp.zeros((d, n), w_in.dtype)
    cols = {}
    off = 0
    for name, sz in (("cq", 384), ("ckv", 256), ("kr", 64), ("gq", 128), ("gk", 128), ("gv", 256), ("glr", 16),
                     ("gg", 256), ("rq", 128), ("rk", 128), ("rv", 256), ("rg", 256)):
        cols[name] = w_in[:, off:off + sz]
        off += sz
    w = jnp.concatenate([
        cols["cq"], cols["ckv"], cols["kr"], z(64), _rot_cols(cols["kr"], 1, MLA_ROPE), z(64),
        cols["gq"], cols["gk"], cols["gv"], cols["gg"], cols["glr"], z(112),
        cols["rq"], cols["rk"], _rot_cols(cols["rq"], RET_HEADS, RET_DK), _rot_cols(cols["rk"], RET_HEADS, RET_DK),
        cols["rv"], cols["rg"]], axis=1).astype(BF16)
    uq = w_uq.reshape(MLA_Q_RANK, MLA_HEADS, MLA_NOPE + MLA_ROPE).transpose(1, 0, 2)
    zq = jnp.zeros((MLA_HEADS, MLA_Q_RANK, 64), w_uq.dtype)
    wq = jnp.concatenate([uq, zq], axis=-1).astype(BF16)
    rope = uq[..., MLA_NOPE:]
    wqr = jnp.concatenate([-rope[..., 32:], rope[..., :32], zq], axis=-1).astype(BF16)
    wkv = w_ukv.reshape(MLA_KV_RANK, MLA_HEADS, MLA_NOPE + MLA_V).transpose(1, 0, 2).astype(BF16)
    wg = jnp.zeros((LANE, LANE), w_gate.dtype).at[:GLA_GATE_RANK].set(w_gate).astype(BF16)
    return w, wq, wqr, wkv, wg, b_gate.reshape(1, LANE)


def kernel(x, attn_norm, w_in, mla_q_norm, mla_w_uq, mla_kv_norm, mla_w_ukv, mla_out_norm, gla_w_gate, gla_b_gate,
           gla_out_norm, ret_out_norm, w_out, ffn_norm, ffn_w_up, ffn_conv_w, ffn_conv_b, ffn_w_down, final_norm):
    batch, seq, d = x.shape
    depth = w_in.shape[0]
    t = batch * seq
    tm_in, tq, tk, tc, tm_ffn, tf = 512, 1024, 512, 512, 512, 256

    mcos, msin = _rope_tables(seq, MLA_ROPE, MLA_ROPE_BASE)
    zpad = jnp.zeros((seq, 64), F32)
    cm = jnp.concatenate([mcos, mcos, zpad], axis=1)
    sm = jnp.concatenate([msin, msin, zpad], axis=1)
    rcos, rsin = _rope_tables(seq, RET_DK, RET_ROPE_BASE)
    cr = jnp.tile(jnp.concatenate([rcos, rcos], axis=1), (1, RET_HEADS))
    sr = jnp.tile(jnp.concatenate([rsin, rsin], axis=1), (1, RET_HEADS))

    xf = x.reshape(t, d)
    for l in range(depth):
        w, wq, wqr, wkv, wg, bg = _pack_in_weights(w_in[l], mla_w_uq[l], mla_w_ukv[l], gla_w_gate[l], gla_b_gate[l])
        q, k, v, lin, la = _in_proj(
            xf, attn_norm[l].reshape(1, d), w, mla_q_norm[l].reshape(1, -1), wq, wqr,
            mla_kv_norm[l].reshape(1, -1), wkv, wg, bg, cm, sm, cr, sr, seq=seq, tm=tm_in)
        ya = _mla(q, k, v, mla_out_norm[l].reshape(MLA_HEADS, 1, MLA_V), batch=batch, seq=seq, tq=tq, tk=tk)
        ybc = _mixers(lin, la, gla_out_norm[l].reshape(1, -1), ret_out_norm[l].reshape(1, -1),
                      batch=batch, seq=seq, tc=tc)
        xf = _out_ffn(
            xf, ya, ybc, w_out[l].astype(BF16), ffn_norm[l].reshape(1, d), ffn_w_up[l].astype(BF16),
            ffn_conv_w[l], ffn_conv_b[l].reshape(1, -1), ffn_w_down[l].astype(BF16), final_norm.reshape(1, d),
            seq=seq, tm=tm_ffn, tf=tf, final=(l == depth - 1))
    return xf.reshape(batch, seq, d)
```

```python
import functools

import jax
import jax.numpy as jnp
from jax import lax
from jax.experimental import pallas as pl
from jax.experimental.pallas import tpu as pltpu

F32 = jnp.float32
BF16 = jnp.bfloat16

D_MODEL = 1024
EPS = 1e-6
MLA_HEADS, MLA_NOPE, MLA_ROPE, MLA_V = 4, 128, 64, 128
MLA_Q_RANK, MLA_KV_RANK = 384, 256
MLA_ROPE_BASE = 10000.0
MLA_QK = 256
GLA_HEADS, GLA_DK, GLA_DV = 4, 32, 64
GLA_GATE_RANK, GLA_GATE_NORM = 16, 16.0
RET_HEADS, RET_DK, RET_DV = 4, 32, 64
RET_ROPE_BASE = 10000.0
CHUNK = 64
D_FF = 2816
LANE = 128
LOG2E = 1.4426950408889634
N_DOWN = 2

W_MLA = 896
W_GLA = 896
W_RET = 1024
W_IN = W_MLA + W_GLA + W_RET
LIN_W = 1536

VMEM_LIMIT = 56 * 1024 * 1024


def _dot(a, b):
    return jnp.dot(a, b, preferred_element_type=F32)


def _dot_nt(a, b):
    return lax.dot_general(a, b, (((1,), (1,)), ((), ())), preferred_element_type=F32)


def _dot_tn(a, b):
    return lax.dot_general(a, b, (((0,), (0,)), ((), ())), preferred_element_type=F32)


def _rms(x, gain):
    return x * lax.rsqrt(jnp.mean(x * x, axis=-1, keepdims=True) + EPS) * gain


def _silu(x):
    return x / (1.0 + jnp.exp(-x))


def _layer_spec(layer, shape, **kw):
    return pl.BlockSpec((None,) + shape, lambda *_: (layer,) + (0,) * len(shape), **kw)


def _in_proj_kernel(x_ref, g_ref, w_ref, qn_ref, wq_ref, wqr_ref, kvn_ref, wkv_ref, wg_ref, bg_ref,
                    cm_ref, sm_ref, cr_ref, sr_ref, q_out, k_out, v_out, lin_out, la_out):
    h = _rms(x_ref[...], g_ref[...]).astype(BF16)
    edges = (0, 1024, 2048, W_IN)
    parts = [_dot(h, w_ref[:, a:b]) for a, b in zip(edges[:-1], edges[1:])]

    def z(a, b):
        p = max(i for i, e in enumerate(edges[:-1]) if e <= a)
        assert b <= edges[p + 1]
        return parts[p][:, a - edges[p]:b - edges[p]]

    cq = _rms(z(0, 384), qn_ref[...]).astype(BF16)
    ckv = _rms(z(384, 640), kvn_ref[...]).astype(BF16)
    cm, sm = cm_ref[...], sm_ref[...]
    kr = (z(640, 768) * cm + z(768, 896) * sm).astype(BF16)
    scale = (MLA_NOPE + MLA_ROPE) ** -0.5 * LOG2E
    ones = jnp.ones((x_ref.shape[0], MLA_V), BF16)
    q_all = _dot(cq, wq_ref[...])
    qrot_all = _dot(cq, wqr_ref[...])
    kv_all = _dot(ckv, wkv_ref[...])
    for hd in range(MLA_HEADS):
        o = hd * MLA_QK
        q_out[hd, :, 0:128] = (q_all[:, o:o + 128] * scale).astype(BF16)
        q_out[hd, :, 128:256] = ((q_all[:, o + 128:o + 256] * cm + qrot_all[:, hd * LANE:(hd + 1) * LANE] * sm)
                                 * scale).astype(BF16)
        k_out[hd, :, 0:128] = kv_all[:, o:o + 128].astype(BF16)
        k_out[hd, :, 128:256] = kr
        v_out[hd, :, 0:128] = kv_all[:, o + 128:o + 256].astype(BF16)
        v_out[hd, :, 128:256] = ones

    o = W_MLA
    lin_out[:, 0:128] = (z(o, o + 128) * (GLA_DK ** -0.5)).astype(BF16)
    lin_out[:, 128:768] = z(o + 128, o + 768).astype(BF16)
    gate = _dot(z(o + 768, o + 896).astype(BF16), wg_ref[...]) + bg_ref[...]
    log_sig = jnp.minimum(gate, 0.0) - jnp.log(1.0 + jnp.exp(-jnp.abs(gate)))
    la_out[...] = log_sig / GLA_GATE_NORM

    o = W_MLA + W_GLA
    cr, sr = cr_ref[...], sr_ref[...]
    lin_out[:, 768:896] = (z(o, o + 128) * cr + z(o + 256, o + 384) * sr).astype(BF16)
    lin_out[:, 896:1024] = ((z(o + 128, o + 256) * cr + z(o + 384, o + 512) * sr) * (RET_DK ** -0.5)).astype(BF16)
    lin_out[:, 1024:1536] = z(o + 512, o + 1024).astype(BF16)


def _in_proj(x, gain, w, qn, wq, wqr, kvn, wkv, wg, bg, cm, sm, cr, sr, *, layer, seq, tm):
    t = x.shape[0]
    n_pos = seq // tm
    full = functools.partial(_layer_spec, layer)
    tab = pl.BlockSpec((tm, LANE), lambda i: (i % n_pos, 0))
    return pl.pallas_call(
        _in_proj_kernel,
        grid=(t // tm,),
        in_specs=[
            pl.BlockSpec((tm, D_MODEL), lambda i: (i, 0)),
            full((1, D_MODEL)), full((D_MODEL, W_IN)),
            full((1, MLA_Q_RANK)), full((MLA_Q_RANK, MLA_HEADS * MLA_QK)), full((MLA_Q_RANK, MLA_HEADS * LANE)),
            full((1, MLA_KV_RANK)), full((MLA_KV_RANK, MLA_HEADS * 256)),
            full((LANE, LANE)), full((1, LANE)),
            tab, tab, tab, tab,
        ],
        out_specs=[
            pl.BlockSpec((MLA_HEADS, tm, MLA_QK), lambda i: (0, i, 0)),
            pl.BlockSpec((MLA_HEADS, tm, MLA_QK), lambda i: (0, i, 0)),
            pl.BlockSpec((MLA_HEADS, tm, 2 * MLA_V), lambda i: (0, i, 0)),
            pl.BlockSpec((tm, LIN_W), lambda i: (i, 0)),
            pl.BlockSpec((tm, LANE), lambda i: (i, 0)),
        ],
        out_shape=[
            jax.ShapeDtypeStruct((MLA_HEADS, t, MLA_QK), BF16),
            jax.ShapeDtypeStruct((MLA_HEADS, t, MLA_QK), BF16),
            jax.ShapeDtypeStruct((MLA_HEADS, t, 2 * MLA_V), BF16),
            jax.ShapeDtypeStruct((t, LIN_W), BF16),
            jax.ShapeDtypeStruct((t, LANE), F32),
        ],
        compiler_params=pltpu.CompilerParams(
            dimension_semantics=("arbitrary",), vmem_limit_bytes=VMEM_LIMIT),
        name="in_proj",
    )(x, gain, w, qn, wq, wqr, kvn, wkv, wg, bg, cm, sm, cr, sr)


def _mla_kernel(q_ref, k_ref, v_ref, g_ref, o_ref, m_ref, acc_ref, s_ref, *, tq, tk):
    qi = pl.program_id(2)
    m_ref[...] = jnp.full(m_ref.shape, -1e30, F32)
    acc_ref[...] = jnp.zeros(acc_ref.shape, F32)

    def scores(j, r0, nr, buf):
        start = pl.multiple_of(j * tk, tk)
        s_ref[buf, r0:r0 + nr, :] = _dot_nt(q_ref[0, r0:r0 + nr, :], k_ref[0, pl.ds(start, tk), :])

    def update(j, r0, nr, buf, diag_col0=None):
        start = pl.multiple_of(j * tk, tk)
        s = s_ref[buf, r0:r0 + nr, :]
        if diag_col0 is not None:
            row = lax.broadcasted_iota(jnp.int32, s.shape, 0) + r0
            col = lax.broadcasted_iota(jnp.int32, s.shape, 1) + diag_col0
            s = jnp.where(col <= row, s, -1e30)
        m_old = m_ref[r0:r0 + nr, :]
        m_new = jnp.maximum(m_old, jnp.max(s, axis=-1, keepdims=True))
        alpha = jnp.exp2(m_old - m_new)
        p = jnp.exp2(s - jnp.concatenate([m_new] * (tk // LANE), axis=1))
        acc_ref[r0:r0 + nr, :] = (jnp.concatenate([alpha, alpha], axis=1) * acc_ref[r0:r0 + nr, :]
                                  + _dot(p.astype(BF16), v_ref[0, pl.ds(start, tk), :]))
        m_ref[r0:r0 + nr, :] = m_new

    scores(0, 0, tq, 0)

    def body(jj, carry):
        j = 2 * jj
        scores(j + 1, 0, tq, 1)
        update(j, 0, tq, 0)
        scores(j + 2, 0, tq, 0)
        update(j + 1, 0, tq, 1)
        return carry

    lax.fori_loop(0, qi, body, 0)
    scores(2 * qi + 1, tk, tk, 1)
    update(2 * qi, 0, tq, 0, diag_col0=0)
    update(2 * qi + 1, tk, tk, 1, diag_col0=tk)
    acc = acc_ref[...]
    o = acc[:, 0:MLA_V] / acc[:, MLA_V:2 * MLA_V]
    o_ref[...] = _rms(o, g_ref[0]).astype(o_ref.dtype)


def _mla(q, k, v, gain, *, layer, batch, seq, tq, tk):
    assert tq == 2 * tk
    nq = seq // tq
    t = batch * seq
    return pl.pallas_call(
        functools.partial(_mla_kernel, tq=tq, tk=tk),
        grid=(MLA_HEADS, batch, nq),
        in_specs=[
            pl.BlockSpec((1, tq, MLA_QK), lambda h, b, i: (h, b * nq + i, 0)),
            pl.BlockSpec((1, seq, MLA_QK), lambda h, b, i: (h, b, 0)),
            pl.BlockSpec((1, seq, 2 * MLA_V), lambda h, b, i: (h, b, 0)),
            pl.BlockSpec((None, 1, 1, MLA_V), lambda h, b, i: (layer, h, 0, 0)),
        ],
        out_specs=pl.BlockSpec((tq, MLA_V), lambda h, b, i: (b * nq + i, h)),
        out_shape=jax.ShapeDtypeStruct((t, MLA_HEADS * MLA_V), BF16),
        scratch_shapes=[pltpu.VMEM((tq, LANE), F32), pltpu.VMEM((tq, 2 * MLA_V), F32),
                        pltpu.VMEM((2, tq, tk), F32)],
        compiler_params=pltpu.CompilerParams(
            dimension_semantics=("arbitrary", "arbitrary", "arbitrary"), vmem_limit_bytes=VMEM_LIMIT),
        name="mla",
    )(q, k, v, gain)


def _split3(x):
    hi = x.astype(BF16)
    r1 = x - hi.astype(F32)
    mid = r1.astype(BF16)
    lo = (r1 - mid.astype(F32)).astype(BF16)
    return hi, mid, lo


def _mm_exact(sel, x):
    hi, mid, lo = _split3(x)
    return _dot(sel, hi) + _dot(sel, mid) + _dot(sel, lo)


def _head_rows(a, n_lanes_per_head):
    lane = lax.broadcasted_iota(jnp.int32, a.shape, 1)
    shift = n_lanes_per_head.bit_length() - 1
    zero = jnp.zeros_like(a)
    return jnp.concatenate(
        [jnp.where((lane >> shift) == hd, a, zero) for hd in range(4)], axis=0)


def _seg_mean64(o2, ind):
    hi = o2.astype(BF16)
    lo = (o2 - hi.astype(F32)).astype(BF16)
    return _dot(hi, ind) + _dot(lo, ind)


def _mixers_kernel(lin_ref, la_ref, gng_ref, rng_ref, y_ref, sg_ref, sr_ref, *, nchunk):
    C = CHUNK

    @pl.when(pl.program_id(1) == 0)
    def _():
        sg_ref[...] = jnp.zeros(sg_ref.shape, F32)
        sr_ref[...] = jnp.zeros(sr_ref.shape, F32)

    def rows(c):
        return slice(c * C, (c + 1) * C)

    def lanes(c):
        return slice(c * LANE, (c + 1) * LANE)

    def wide(a):
        return jnp.concatenate([a[rows(c)] for c in range(nchunk)], axis=1)

    r_cc = lax.broadcasted_iota(jnp.int32, (C, C), 0)
    c_cc = lax.broadcasted_iota(jnp.int32, (C, C), 1)
    row_w = lax.broadcasted_iota(jnp.int32, (C, nchunk * LANE), 0)
    i_a = lax.broadcasted_iota(jnp.int32, (C, 256), 0)
    j_a = lax.broadcasted_iota(jnp.int32, (C, 256), 1) & (C - 1)
    h_a = lax.broadcasted_iota(jnp.int32, (C, 256), 1) >> 6
    sm = (lax.broadcasted_iota(jnp.int32, (256, LANE), 0) >> 6) == (
        lax.broadcasted_iota(jnp.int32, (256, LANE), 1) >> 5)
    r256 = lax.broadcasted_iota(jnp.int32, (256, 256), 0)
    c256 = lax.broadcasted_iota(jnp.int32, (256, 256), 1)
    ind = jnp.where((r256 >> 6) == (c256 >> 6), 1.0 / 64, 0.0).astype(BF16)

    q_w = wide(lin_ref[:, 0:128].astype(F32))
    k_w = wide(lin_ref[:, 128:256].astype(F32))
    g_w = wide(la_ref[...])
    tri = jnp.where(c_cc <= r_cc, 1.0, 0.0).astype(BF16)
    b_w = _mm_exact(tri, g_w)
    b_end = b_w[C - 1:C, :]

    a_cat = [jnp.zeros((C, 256), F32) for _ in range(nchunk)]

    def add_level(ql, kl, keep):
        for c in range(nchunk):
            kst = _head_rows(kl[:, lanes(c)], GLA_DK).astype(BF16)
            p = _dot_nt(ql[:, lanes(c)].astype(BF16), kst)
            a_cat[c] = a_cat[c] + jnp.where(keep, p, 0.0)

    def boundary(s):
        g = 2 * s
        if g >= 8:
            return jnp.concatenate(
                [jnp.broadcast_to(b_w[m * g + s - 1:m * g + s, :], (g, b_w.shape[1])) for m in range(C // g)], axis=0)
        pos = row_w & (g - 1)
        out = b_w
        for o in range(g):
            if o != s - 1:
                out = jnp.where(pos == o, pltpu.roll(b_w, (o - (s - 1)) % C, axis=0), out)
        return out

    for s in (1, 2, 4, 8, 16, 32):
        sh = s.bit_length() - 1
        bref = boundary(s)
        odd = ((row_w >> sh) & 1) == 1
        qf = jnp.where(odd, jnp.exp(jnp.minimum(b_w - bref, 0.0)), 0.0)
        kf = jnp.where(odd, 0.0, jnp.exp(jnp.minimum(bref - b_w, 0.0)))
        add_level(q_w * qf, k_w * kf, (i_a >> (sh + 1)) == (j_a >> (sh + 1)))
    add_level(q_w, k_w, i_a == j_a)

    qd_w = (q_w * jnp.exp(b_w)).astype(BF16)
    kd_w = (k_w * jnp.exp(b_end - b_w)).astype(BF16)
    dec_w = jnp.exp(b_end)
    state = sg_ref[...]
    outs = []
    for c in range(nchunk):
        vc = lin_ref[rows(c), 256:512]
        intra = _dot(a_cat[c].astype(BF16), _head_rows(vc, GLA_DV))
        inter = _dot_nt(qd_w[:, lanes(c)], state.astype(BF16))
        upd = _dot_tn(vc, kd_w[:, lanes(c)])
        state = state * dec_w[:, lanes(c)] + jnp.where(sm, upd, 0.0)
        outs.append(inter + intra)
    sg_ref[...] = state
    o = jnp.concatenate(outs, axis=0)
    o = o * lax.rsqrt(_seg_mean64(o * o, ind) + EPS) * gng_ref[...]
    y_ref[:, 0:256] = (o * _silu(lin_ref[:, 512:768].astype(F32))).astype(y_ref.dtype)

    def log_gamma(hd):
        v = jnp.where(hd == 0, 2.0 ** -5, jnp.where(hd == 1, 2.0 ** -6, jnp.where(hd == 2, 2.0 ** -7, 2.0 ** -8)))
        return jnp.log(1.0 - v)

    lg_a = log_gamma(h_a)
    diff = (i_a - j_a).astype(F32)
    d_cat = jnp.where(diff >= 0, jnp.exp(lg_a * jnp.maximum(diff, 0.0)), 0.0)
    row_c = lax.broadcasted_iota(jnp.int32, (C, LANE), 0).astype(F32)
    lg_c = log_gamma(lax.broadcasted_iota(jnp.int32, (C, LANE), 1) >> 5)
    k_dec = jnp.exp(lg_c * (C - 1.0 - row_c))
    q_dec = jnp.exp(lg_c * (row_c + 1.0))
    c_dec = jnp.exp(lg_c[0:1, :] * float(C))
    state = sr_ref[...]
    outs = []
    for c in range(nchunk):
        qc = lin_ref[rows(c), 768:896]
        kc = lin_ref[rows(c), 896:1024]
        vc = lin_ref[rows(c), 1024:1280]
        p = _dot_nt(qc, _head_rows(kc, RET_DK)) * d_cat
        intra = _dot(p.astype(BF16), _head_rows(vc, RET_DV))
        inter = _dot_nt((qc.astype(F32) * q_dec).astype(BF16), state.astype(BF16))
        upd = _dot_tn(vc, (kc.astype(F32) * k_dec).astype(BF16))
        state = state * c_dec + jnp.where(sm, upd, 0.0)
        outs.append(inter + intra)
    sr_ref[...] = state
    o = jnp.concatenate(outs, axis=0)
    o = o * lax.rsqrt(_seg_mean64(o * o, ind) + EPS) * rng_ref[...]
    y_ref[:, 256:512] = (o * _silu(lin_ref[:, 1280:1536].astype(F32))).astype(y_ref.dtype)


def _mixers(lin, la, gla_gain, ret_gain, *, layer, batch, seq, tc):
    n = seq // tc
    t = batch * seq
    return pl.pallas_call(
        functools.partial(_mixers_kernel, nchunk=tc // CHUNK),
        grid=(batch, n),
        in_specs=[
            pl.BlockSpec((tc, LIN_W), lambda b, c: (b * n + c, 0)),
            pl.BlockSpec((tc, LANE), lambda b, c: (b * n + c, 0)),
            _layer_spec(layer, (1, 256)),
            _layer_spec(layer, (1, 256)),
        ],
        out_specs=pl.BlockSpec((tc, 512), lambda b, c: (b * n + c, 0)),
        out_shape=jax.ShapeDtypeStruct((t, 512), BF16),
        scratch_shapes=[pltpu.VMEM((256, LANE), F32), pltpu.VMEM((256, LANE), F32)],
        compiler_params=pltpu.CompilerParams(
            dimension_semantics=("arbitrary", "arbitrary"), vmem_limit_bytes=VMEM_LIMIT),
        name="mixers",
    )(lin, la, gla_gain, ret_gain)


def _out_ffn_kernel(x_ref, ya_ref, ybc_ref, wo_ref, fg_ref, wup_ref, cw_ref, cb_ref, wd_ref, ng_ref,
                    o_ref, abuf_ref, act_ref, *, tm, tf, n_down, n_pos, final):
    i = pl.program_id(0)
    xn = x_ref[...] + _dot(ya_ref[...], wo_ref[0:512, :]) + _dot(ybc_ref[...], wo_ref[512:1024, :])
    hn = _rms(xn, fg_ref[...]).astype(BF16)

    @pl.when(i % n_pos == 0)
    def _():
        abuf_ref[0:8, :] = jnp.zeros((8, D_FF), F32)

    @pl.when(i % n_pos != 0)
    def _():
        abuf_ref[0:8, :] = abuf_ref[tm:tm + 8, :]

    nf = D_FF // tf

    def up(f):
        c0, c1 = f * tf, (f + 1) * tf
        a = _dot(hn, wup_ref[:, c0:c1])
        abuf_ref[8:tm + 8, c0:c1] = a
        return a, _dot(hn, wup_ref[:, D_FF + c0:D_FF + c1])

    bounds = [round(g * nf / n_down) for g in range(n_down + 1)]
    down_after = {bounds[g + 1]: (bounds[g] * tf, bounds[g + 1] * tf) for g in range(n_down)}
    acc = None
    ahead = 2
    pending = [up(f) for f in range(ahead)]
    for f in range(nf):
        c0, c1 = f * tf, (f + 1) * tf
        a, bv = pending.pop(0)
        if f + ahead < nf:
            pending.append(up(f + ahead))
        cw = cw_ref[:, c0:c1]
        conv = (cb_ref[:, c0:c1] + cw[0:1, :] * abuf_ref[6:tm + 6, c0:c1]
                + cw[1:2, :] * abuf_ref[7:tm + 7, c0:c1] + cw[2:3, :] * a)
        act_ref[:, c0:c1] = (_silu(conv) * bv).astype(BF16)
        if f + 1 in down_after:
            k0, k1 = down_after[f + 1]
            down = _dot(act_ref[:, k0:k1], wd_ref[k0:k1, :])
            acc = down if acc is None else acc + down
    acc = acc + xn
    if final:
        acc = _rms(acc, ng_ref[...])
    o_ref[...] = acc


def _out_ffn(x, ya, ybc, wo, fgain, wup, cw, cb, wd, ngain, *, layer, seq, tm, tf, final):
    t = x.shape[0]
    once = functools.partial(_layer_spec, layer, pipeline_mode=pl.Buffered(1))
    return pl.pallas_call(
        functools.partial(_out_ffn_kernel, tm=tm, tf=tf, n_down=N_DOWN, n_pos=seq // tm, final=final),
        grid=(t // tm,),
        in_specs=[
            pl.BlockSpec((tm, D_MODEL), lambda i: (i, 0)),
            pl.BlockSpec((tm, 512), lambda i: (i, 0)),
            pl.BlockSpec((tm, 512), lambda i: (i, 0)),
            once((D_MODEL, D_MODEL)), once((1, D_MODEL)), once((D_MODEL, 2 * D_FF)),
            once((3, D_FF)), once((1, D_FF)), once((D_FF, D_MODEL)),
            pl.BlockSpec((1, D_MODEL), lambda i: (0, 0)),
        ],
        out_specs=pl.BlockSpec((tm, D_MODEL), lambda i: (i, 0)),
        out_shape=jax.ShapeDtypeStruct((t, D_MODEL), F32),
        scratch_shapes=[pltpu.VMEM((tm + 8, D_FF), F32), pltpu.VMEM((tm, D_FF), BF16)],
        compiler_params=pltpu.CompilerParams(dimension_semantics=("arbitrary",), vmem_limit_bytes=VMEM_LIMIT),
        name="out_ffn",
    )(x, ya, ybc, wo, fgain, wup, cw, cb, wd, ngain)


def _rot_cols(w, heads, dim):
    w4 = w.reshape(w.shape[:-1] + (heads, dim))
    half = dim // 2
    return jnp.concatenate([-w4[..., half:], w4[..., :half]], axis=-1).reshape(w.shape)


def _rope_tables(seq, dim, base):
    inv = base ** (-(jnp.arange(0, dim, 2, dtype=F32) / dim))
    ang = jnp.arange(seq, dtype=F32)[:, None] * inv[None, :]
    return jnp.cos(ang), jnp.sin(ang)


def _pack_in_weights(w_in, w_uq, w_ukv, w_gate):
    depth, d, _ = w_in.shape
    z = lambda n: jnp.zeros((depth, d, n), w_in.dtype)
    cols = {}
    off = 0
    for name, sz in (("cq", 384), ("ckv", 256), ("kr", 64), ("gq", 128), ("gk", 128), ("gv", 256), ("glr", 16),
                     ("gg", 256), ("rq", 128), ("rk", 128), ("rv", 256), ("rg", 256)):
        cols[name] = w_in[..., off:off + sz]
        off += sz
    w = jnp.concatenate([
        cols["cq"], cols["ckv"], cols["kr"], z(64), _rot_cols(cols["kr"], 1, MLA_ROPE), z(64),
        cols["gq"], cols["gk"], cols["gv"], cols["gg"], cols["glr"], z(112),
        cols["rq"], cols["rk"], _rot_cols(cols["rq"], RET_HEADS, RET_DK), _rot_cols(cols["rk"], RET_HEADS, RET_DK),
        cols["rv"], cols["rg"]], axis=-1).astype(BF16)
    uq = w_uq.reshape(depth, MLA_Q_RANK, MLA_HEADS, MLA_NOPE + MLA_ROPE)
    zq = jnp.zeros((depth, MLA_Q_RANK, MLA_HEADS, 64), w_uq.dtype)
    wq = jnp.concatenate([uq, zq], axis=-1).reshape(depth, MLA_Q_RANK, MLA_HEADS * MLA_QK).astype(BF16)
    rope = uq[..., MLA_NOPE:]
    wqr = jnp.concatenate([-rope[..., 32:], rope[..., :32], zq], axis=-1)
    wqr = wqr.reshape(depth, MLA_Q_RANK, MLA_HEADS * LANE).astype(BF16)
    wg = jnp.zeros((depth, LANE, LANE), w_gate.dtype).at[:, :GLA_GATE_RANK].set(w_gate).astype(BF16)
    return w, wq, wqr, w_ukv.astype(BF16), wg


def kernel(x, attn_norm, w_in, mla_q_norm, mla_w_uq, mla_kv_norm, mla_w_ukv, mla_out_norm, gla_w_gate, gla_b_gate,
           gla_out_norm, ret_out_norm, w_out, ffn_norm, ffn_w_up, ffn_conv_w, ffn_conv_b, ffn_w_down, final_norm):
    batch, seq, d = x.shape
    depth = w_in.shape[0]
    t = batch * seq
    tm_in, tq, tk, tc, tm_ffn, tf = 512, 1024, 512, 512, 512, 256

    mcos, msin = _rope_tables(seq, MLA_ROPE, MLA_ROPE_BASE)
    zpad = jnp.zeros((seq, 64), F32)
    cm = jnp.concatenate([mcos, mcos, zpad], axis=1)
    sm = jnp.concatenate([msin, msin, zpad], axis=1)
    rcos, rsin = _rope_tables(seq, RET_DK, RET_ROPE_BASE)
    cr = jnp.tile(jnp.concatenate([rcos, rcos], axis=1), (1, RET_HEADS))
    sr = jnp.tile(jnp.concatenate([rsin, rsin], axis=1), (1, RET_HEADS))

    row = lambda a: a.reshape(depth, 1, -1)
    w, wq, wqr, wkv, wg = _pack_in_weights(w_in, mla_w_uq, mla_w_ukv, gla_w_gate)
    attn_g, q_g, kv_g, b_gate = row(attn_norm), row(mla_q_norm), row(mla_kv_norm), row(gla_b_gate)
    mla_g = mla_out_norm.reshape(depth, MLA_HEADS, 1, MLA_V)
    gla_g, ret_g, ffn_g, conv_b = row(gla_out_norm), row(ret_out_norm), row(ffn_norm), row(ffn_conv_b)
    wo, wup, wd = w_out.astype(BF16), ffn_w_up.astype(BF16), ffn_w_down.astype(BF16)

    xf = x.reshape(t, d)
    for l in range(depth):
        q, k, v, lin, la = _in_proj(xf, attn_g, w, q_g, wq, wqr, kv_g, wkv, wg, b_gate, cm, sm, cr, sr,
                                    layer=l, seq=seq, tm=tm_in)
        ya = _mla(q, k, v, mla_g, layer=l, batch=batch, seq=seq, tq=tq, tk=tk)
        ybc = _mixers(lin, la, gla_g, ret_g, layer=l, batch=batch, seq=seq, tc=tc)
        xf = _out_ffn(xf, ya, ybc, wo, ffn_g, wup, ffn_conv_w, conv_b, wd, final_norm.reshape(1, d),
                      layer=l, seq=seq, tm=tm_ffn, tf=tf, final=(l == depth - 1))
    return xf.reshape(batch, seq, d)
```

```python
import functools

import jax
import jax.numpy as jnp
from jax import lax
from jax.experimental import pallas as pl
from jax.experimental.pallas import tpu as pltpu

F32 = jnp.float32
BF16 = jnp.bfloat16

D_MODEL = 1024
EPS = 1e-6
MLA_HEADS, MLA_NOPE, MLA_ROPE, MLA_V = 4, 128, 64, 128
MLA_Q_RANK, MLA_KV_RANK = 384, 256
MLA_ROPE_BASE = 10000.0
MLA_QK = 256
GLA_HEADS, GLA_DK, GLA_DV = 4, 32, 64
GLA_GATE_RANK, GLA_GATE_NORM = 16, 16.0
RET_HEADS, RET_DK, RET_DV = 4, 32, 64
RET_ROPE_BASE = 10000.0
CHUNK = 64
D_FF = 2816
LANE = 128
LOG2E = 1.4426950408889634
N_DOWN = 2

W_EDGES = (0, 1024, 2048, 2432)
W_IN = W_EDGES[-1]
LIN_W = 1536

VMEM_LIMIT = 56 * 1024 * 1024


def _dot(a, b):
    return jnp.dot(a, b, preferred_element_type=F32)


def _dot_nt(a, b):
    return lax.dot_general(a, b, (((1,), (1,)), ((), ())), preferred_element_type=F32)


def _dot_tn(a, b):
    return lax.dot_general(a, b, (((0,), (0,)), ((), ())), preferred_element_type=F32)


def _rms(x, gain):
    return x * lax.rsqrt(jnp.mean(x * x, axis=-1, keepdims=True) + EPS) * gain


def _silu(x):
    h = 0.5 * x
    return h + h * jnp.tanh(h)


def _rot_half(x, half):
    lane = lax.broadcasted_iota(jnp.int32, x.shape, 1)
    first = (lane & (2 * half - 1)) < half
    return jnp.where(first, -pltpu.roll(x, LANE - half, axis=1), pltpu.roll(x, half, axis=1))


def _layer_spec(layer, shape, **kw):
    return pl.BlockSpec((None,) + shape, lambda *_: (layer,) + (0,) * len(shape), **kw)


def _in_proj_kernel(x_ref, g_ref, w_ref, qn_ref, wq_ref, kvn_ref, wkv_ref, wg_ref, bg_ref,
                    cm_ref, sm_ref, cr_ref, sr_ref, q_out, k_out, v_out, lin_out, la_out):
    h = _rms(x_ref[...], g_ref[...]).astype(BF16)
    p0, p1, p2 = [_dot(h, w_ref[:, a:b]) for a, b in zip(W_EDGES[:-1], W_EDGES[1:])]

    cq = _rms(p0[:, 0:384], qn_ref[...]).astype(BF16)
    ckv = _rms(p0[:, 384:640], kvn_ref[...]).astype(BF16)
    cm, sm = cm_ref[...], sm_ref[...]
    kr = p0[:, 640:768]
    kr = (kr * cm + _rot_half(kr, MLA_ROPE // 2) * sm).astype(BF16)
    scale = (MLA_NOPE + MLA_ROPE) ** -0.5 * LOG2E
    ones = jnp.ones((x_ref.shape[0], MLA_V), BF16)
    q_all = _dot(cq, wq_ref[...])
    kv_all = _dot(ckv, wkv_ref[...])
    for hd in range(MLA_HEADS):
        o = hd * MLA_QK
        qr = q_all[:, o + 128:o + 256]
        q_out[hd, :, 0:128] = (q_all[:, o:o + 128] * scale).astype(BF16)
        q_out[hd, :, 128:256] = ((qr * cm + _rot_half(qr, MLA_ROPE // 2) * sm) * scale).astype(BF16)
        k_out[hd, :, 0:128] = kv_all[:, o:o + 128].astype(BF16)
        k_out[hd, :, 128:256] = kr
        v_out[hd, :, 0:128] = kv_all[:, o + 128:o + 256].astype(BF16)
        v_out[hd, :, 128:256] = ones

    lin_out[:, 0:128] = (p0[:, 768:896] * (GLA_DK ** -0.5)).astype(BF16)
    lin_out[:, 128:256] = p0[:, 896:1024].astype(BF16)
    lin_out[:, 256:768] = p1[:, 0:512].astype(BF16)
    gate = _dot(p2[:, 0:128].astype(BF16), wg_ref[...]) + bg_ref[...]
    log_sig = jnp.minimum(gate, 0.0) - jnp.log(1.0 + jnp.exp(-jnp.abs(gate)))
    la_out[...] = log_sig / GLA_GATE_NORM

    cr, sr = cr_ref[...], sr_ref[...]
    rq, rk = p2[:, 128:256], p2[:, 256:384]
    lin_out[:, 768:896] = (rq * cr + _rot_half(rq, RET_DK // 2) * sr).astype(BF16)
    lin_out[:, 896:1024] = ((rk * cr + _rot_half(rk, RET_DK // 2) * sr) * (RET_DK ** -0.5)).astype(BF16)
    lin_out[:, 1024:1536] = p1[:, 512:1024].astype(BF16)


def _in_proj(x, gain, w, qn, wq, kvn, wkv, wg, bg, cm, sm, cr, sr, *, layer, seq, tm):
    t = x.shape[0]
    n_pos = seq // tm
    full = functools.partial(_layer_spec, layer)
    tab = pl.BlockSpec((tm, LANE), lambda i: (i % n_pos, 0))
    return pl.pallas_call(
        _in_proj_kernel,
        grid=(t // tm,),
        in_specs=[
            pl.BlockSpec((tm, D_MODEL), lambda i: (i, 0)),
            full((1, D_MODEL)), full((D_MODEL, W_IN)),
            full((1, MLA_Q_RANK)), full((MLA_Q_RANK, MLA_HEADS * MLA_QK)),
            full((1, MLA_KV_RANK)), full((MLA_KV_RANK, MLA_HEADS * 256)),
            full((LANE, LANE)), full((1, LANE)),
            tab, tab, tab, tab,
        ],
        out_specs=[
            pl.BlockSpec((MLA_HEADS, tm, MLA_QK), lambda i: (0, i, 0)),
            pl.BlockSpec((MLA_HEADS, tm, MLA_QK), lambda i: (0, i, 0)),
            pl.BlockSpec((MLA_HEADS, tm, 2 * MLA_V), lambda i: (0, i, 0)),
            pl.BlockSpec((tm, LIN_W), lambda i: (i, 0)),
            pl.BlockSpec((tm, LANE), lambda i: (i, 0)),
        ],
        out_shape=[
            jax.ShapeDtypeStruct((MLA_HEADS, t, MLA_QK), BF16),
            jax.ShapeDtypeStruct((MLA_HEADS, t, MLA_QK), BF16),
            jax.ShapeDtypeStruct((MLA_HEADS, t, 2 * MLA_V), BF16),
            jax.ShapeDtypeStruct((t, LIN_W), BF16),
            jax.ShapeDtypeStruct((t, LANE), F32),
        ],
        compiler_params=pltpu.CompilerParams(
            dimension_semantics=("arbitrary",), vmem_limit_bytes=VMEM_LIMIT),
        name="in_proj",
    )(x, gain, w, qn, wq, kvn, wkv, wg, bg, cm, sm, cr, sr)


def _mla_kernel(q_ref, k_ref, v_ref, g_ref, o_ref, m_ref, acc_ref, s_ref, *, tq, tk, nq):
    def scores(qi, j, r0, nr, buf):
        start = pl.multiple_of(j * tk, tk)
        q = q_ref[0, qi * tq + r0:qi * tq + r0 + nr, :]
        s_ref[2 * (qi & 1) + buf, r0:r0 + nr, :] = _dot_nt(q, k_ref[0, pl.ds(start, tk), :])

    def update(qi, j, r0, nr, buf, diag_col0=None):
        par = qi & 1
        start = pl.multiple_of(j * tk, tk)
        s = s_ref[2 * par + buf, r0:r0 + nr, :]
        if diag_col0 is not None:
            row = lax.broadcasted_iota(jnp.int32, s.shape, 0) + r0
            col = lax.broadcasted_iota(jnp.int32, s.shape, 1) + diag_col0
            s = jnp.where(col <= row, s, -1e30)
        m_old = m_ref[par, r0:r0 + nr, :]
        m_new = jnp.maximum(m_old, jnp.max(s, axis=-1, keepdims=True))
        alpha = jnp.exp2(m_old - m_new)
        p = jnp.exp2(s - jnp.concatenate([m_new] * (tk // LANE), axis=1))
        acc_ref[par, r0:r0 + nr, :] = (jnp.concatenate([alpha, alpha], axis=1) * acc_ref[par, r0:r0 + nr, :]
                                       + _dot(p.astype(BF16), v_ref[0, pl.ds(start, tk), :]))
        m_ref[par, r0:r0 + nr, :] = m_new

    for qi in range(nq):
        par = qi & 1
        m_ref[par] = jnp.full(m_ref.shape[1:], -1e30, F32)
        acc_ref[par] = jnp.zeros(acc_ref.shape[1:], F32)
        scores(qi, 0, 0, tq, 0)

        def body(jj, carry, qi=qi):
            j = 2 * jj
            scores(qi, j + 1, 0, tq, 1)
            update(qi, j, 0, tq, 0)
            scores(qi, j + 2, 0, tq, 0)
            update(qi, j + 1, 0, tq, 1)
            return carry

        if qi:
            lax.fori_loop(0, qi, body, 0)
        scores(qi, 2 * qi + 1, tk, tk, 1)
        update(qi, 2 * qi, 0, tq, 0, diag_col0=0)
        update(qi, 2 * qi + 1, tk, tk, 1, diag_col0=tk)
        acc = acc_ref[par]
        o = acc[:, 0:MLA_V] / acc[:, MLA_V:2 * MLA_V]
        o_ref[qi * tq:(qi + 1) * tq, :] = _rms(o, g_ref[0]).astype(o_ref.dtype)


def _mla(q, k, v, gain, *, layer, batch, seq, tq, tk):
    assert tq == 2 * tk
    t = batch * seq
    return pl.pallas_call(
        functools.partial(_mla_kernel, tq=tq, tk=tk, nq=seq // tq),
        grid=(MLA_HEADS, batch),
        in_specs=[
            pl.BlockSpec((1, seq, MLA_QK), lambda h, b: (h, b, 0)),
            pl.BlockSpec((1, seq, MLA_QK), lambda h, b: (h, b, 0)),
            pl.BlockSpec((1, seq, 2 * MLA_V), lambda h, b: (h, b, 0)),
            pl.BlockSpec((None, 1, 1, MLA_V), lambda h, b: (layer, h, 0, 0)),
        ],
        out_specs=pl.BlockSpec((seq, MLA_V), lambda h, b: (b, h)),
        out_shape=jax.ShapeDtypeStruct((t, MLA_HEADS * MLA_V), BF16),
        scratch_shapes=[pltpu.VMEM((2, tq, LANE), F32), pltpu.VMEM((2, tq, 2 * MLA_V), F32),
                        pltpu.VMEM((4, tq, tk), F32)],
        compiler_params=pltpu.CompilerParams(
            dimension_semantics=("arbitrary", "arbitrary"), vmem_limit_bytes=VMEM_LIMIT),
        name="mla",
    )(q, k, v, gain)


def _mm_split2(sel, x):
    hi = x.astype(BF16)
    lo = (x - hi.astype(F32)).astype(BF16)
    return _dot(sel, hi) + _dot(sel, lo)


def _head_rows(a, n_lanes_per_head):
    lane = lax.broadcasted_iota(jnp.int32, a.shape, 1)
    shift = n_lanes_per_head.bit_length() - 1
    zero = jnp.zeros_like(a)
    return jnp.concatenate(
        [jnp.where((lane >> shift) == hd, a, zero) for hd in range(4)], axis=0)


def _mixers_kernel(lin_ref, la_ref, gng_ref, rng_ref, y_ref, sg_ref, sr_ref, *, nchunk):
    C = CHUNK

    @pl.when(pl.program_id(1) == 0)
    def _():
        sg_ref[...] = jnp.zeros(sg_ref.shape, F32)
        sr_ref[...] = jnp.zeros(sr_ref.shape, F32)

    def rows(c):
        return slice(c * C, (c + 1) * C)

    def lanes(c):
        return slice(c * LANE, (c + 1) * LANE)

    def wide(a):
        return jnp.concatenate([a[rows(c)] for c in range(nchunk)], axis=1)

    r_cc = lax.broadcasted_iota(jnp.int32, (C, C), 0)
    c_cc = lax.broadcasted_iota(jnp.int32, (C, C), 1)
    row_w = lax.broadcasted_iota(jnp.int32, (C, nchunk * LANE), 0)
    i_a = lax.broadcasted_iota(jnp.int32, (C, 256), 0)
    j_a = lax.broadcasted_iota(jnp.int32, (C, 256), 1) & (C - 1)
    h_a = lax.broadcasted_iota(jnp.int32, (C, 256), 1) >> 6
    sm = (lax.broadcasted_iota(jnp.int32, (256, LANE), 0) >> 6) == (
        lax.broadcasted_iota(jnp.int32, (256, LANE), 1) >> 5)
    r256 = lax.broadcasted_iota(jnp.int32, (256, 256), 0)
    c256 = lax.broadcasted_iota(jnp.int32, (256, 256), 1)
    ind = jnp.where((r256 >> 6) == (c256 >> 6), 1.0 / 64, 0.0).astype(BF16)

    q_w = wide(lin_ref[:, 0:128].astype(F32))
    k_w = wide(lin_ref[:, 128:256].astype(F32))
    g_w = wide(la_ref[...])
    tri = jnp.where(c_cc <= r_cc, 1.0, 0.0).astype(BF16)
    b_w = _mm_split2(tri, g_w)
    b_end = b_w[C - 1:C, :]

    a_cat = [None] * nchunk

    def put_level(ql, kl, keep):
        ql, kl = ql.astype(BF16), kl.astype(BF16)
        for c in range(nchunk):
            p = _dot_nt(ql[:, lanes(c)], _head_rows(kl[:, lanes(c)], GLA_DK))
            a_cat[c] = p if a_cat[c] is None else jnp.where(keep, p, a_cat[c])

    def boundary(s):
        g = 2 * s
        if g >= 8:
            return jnp.concatenate(
                [jnp.broadcast_to(b_w[m * g + s - 1:m * g + s, :], (g, b_w.shape[1])) for m in range(C // g)], axis=0)
        pos = row_w & (g - 1)
        out = b_w
        for o in range(g):
            if o != s - 1:
                out = jnp.where(pos == o, pltpu.roll(b_w, (o - (s - 1)) % C, axis=0), out)
        return out

    for s in (32, 16, 8, 4, 2, 1):
        sh = s.bit_length() - 1
        odd = ((row_w >> sh) & 1) == 1
        e = jnp.exp(-jnp.abs(b_w - boundary(s)))
        put_level(jnp.where(odd, q_w * e, 0.0), jnp.where(odd, 0.0, k_w * e), (i_a >> (sh + 1)) == (j_a >> (sh + 1)))
    put_level(q_w, k_w, i_a == j_a)

    qd_w = (q_w * jnp.exp(b_w)).astype(BF16)
    kd_w = (k_w * jnp.exp(b_end - b_w)).astype(BF16)
    dec_w = jnp.exp(b_end)
    state = sg_ref[...]
    outs = []
    for c in range(nchunk):
        vc = lin_ref[rows(c), 256:512]
        intra = _dot(a_cat[c].astype(BF16), _head_rows(vc, GLA_DV))
        inter = _dot_nt(qd_w[:, lanes(c)], state.astype(BF16))
        upd = _dot_tn(vc, kd_w[:, lanes(c)])
        state = state * dec_w[:, lanes(c)] + jnp.where(sm, upd, 0.0)
        outs.append(inter + intra)
    sg_ref[...] = state
    o = jnp.concatenate(outs, axis=0)
    o = o * lax.rsqrt(_dot((o * o).astype(BF16), ind) + EPS) * gng_ref[...]
    y_ref[:, 0:256] = (o * _silu(lin_ref[:, 512:768].astype(F32))).astype(y_ref.dtype)

    def log_gamma(hd):
        v = jnp.where(hd == 0, 2.0 ** -5, jnp.where(hd == 1, 2.0 ** -6, jnp.where(hd == 2, 2.0 ** -7, 2.0 ** -8)))
        return jnp.log(1.0 - v)

    lg_a = log_gamma(h_a)
    diff = (i_a - j_a).astype(F32)
    d_cat = jnp.where(diff >= 0, jnp.exp(lg_a * jnp.maximum(diff, 0.0)), 0.0)
    row_c = lax.broadcasted_iota(jnp.int32, (C, LANE), 0).astype(F32)
    lg_c = log_gamma(lax.broadcasted_iota(jnp.int32, (C, LANE), 1) >> 5)
    k_dec = jnp.exp(lg_c * (C - 1.0 - row_c))
    q_dec = jnp.exp(lg_c * (row_c + 1.0))
    c_dec = jnp.exp(lg_c[0:1, :] * float(C))
    state = sr_ref[...]
    outs = []
    for c in range(nchunk):
        qc = lin_ref[rows(c), 768:896]
        kc = lin_ref[rows(c), 896:1024]
        vc = lin_ref[rows(c), 1024:1280]
        p = _dot_nt(qc, _head_rows(kc, RET_DK)) * d_cat
        intra = _dot(p.astype(BF16), _head_rows(vc, RET_DV))
        inter = _dot_nt((qc.astype(F32) * q_dec).astype(BF16), state.astype(BF16))
        upd = _dot_tn(vc, (kc.astype(F32) * k_dec).astype(BF16))
        state = state * c_dec + jnp.where(sm, upd, 0.0)
        outs.append(inter + intra)
    sr_ref[...] = state
    o = jnp.concatenate(outs, axis=0)
    o = o * lax.rsqrt(_dot((o * o).astype(BF16), ind) + EPS) * rng_ref[...]
    y_ref[:, 256:512] = (o * _silu(lin_ref[:, 1280:1536].astype(F32))).astype(y_ref.dtype)


def _mixers(lin, la, gla_gain, ret_gain, *, layer, batch, seq, tc):
    n = seq // tc
    t = batch * seq
    return pl.pallas_call(
        functools.partial(_mixers_kernel, nchunk=tc // CHUNK),
        grid=(batch, n),
        in_specs=[
            pl.BlockSpec((tc, LIN_W), lambda b, c: (b * n + c, 0)),
            pl.BlockSpec((tc, LANE), lambda b, c: (b * n + c, 0)),
            _layer_spec(layer, (1, 256)),
            _layer_spec(layer, (1, 256)),
        ],
        out_specs=pl.BlockSpec((tc, 512), lambda b, c: (b * n + c, 0)),
        out_shape=jax.ShapeDtypeStruct((t, 512), BF16),
        scratch_shapes=[pltpu.VMEM((256, LANE), F32), pltpu.VMEM((256, LANE), F32)],
        compiler_params=pltpu.CompilerParams(
            dimension_semantics=("arbitrary", "arbitrary"), vmem_limit_bytes=VMEM_LIMIT),
        name="mixers",
    )(lin, la, gla_gain, ret_gain)


def _out_ffn_kernel(x_ref, ya_ref, ybc_ref, wo_ref, fg_ref, wup_ref, cw_ref, cb_ref, wd_ref, ng_ref,
                    o_ref, abuf_ref, act_ref, *, tm, tf, n_down, n_pos, final):
    i = pl.program_id(0)
    xn = x_ref[...] + _dot(ya_ref[...], wo_ref[0:512, :]) + _dot(ybc_ref[...], wo_ref[512:1024, :])
    hn = _rms(xn, fg_ref[...]).astype(BF16)

    @pl.when(i % n_pos == 0)
    def _():
        abuf_ref[0:8, :] = jnp.zeros((8, D_FF), F32)

    @pl.when(i % n_pos != 0)
    def _():
        abuf_ref[0:8, :] = abuf_ref[tm:tm + 8, :]

    nf = D_FF // tf

    def up(f):
        c0, c1 = f * tf, (f + 1) * tf
        a = _dot(hn, wup_ref[:, c0:c1])
        abuf_ref[8:tm + 8, c0:c1] = a
        return a, _dot(hn, wup_ref[:, D_FF + c0:D_FF + c1])

    bounds = [round(g * nf / n_down) for g in range(n_down + 1)]
    down_after = {bounds[g + 1]: (bounds[g] * tf, bounds[g + 1] * tf) for g in range(n_down)}
    acc = None
    ahead = 2
    pending = [up(f) for f in range(ahead)]
    for f in range(nf):
        c0, c1 = f * tf, (f + 1) * tf
        a, bv = pending.pop(0)
        if f + ahead < nf:
            pending.append(up(f + ahead))
        cw = cw_ref[:, c0:c1]
        conv = (cb_ref[:, c0:c1] + cw[0:1, :] * abuf_ref[6:tm + 6, c0:c1]
                + cw[1:2, :] * abuf_ref[7:tm + 7, c0:c1] + cw[2:3, :] * a)
        act_ref[:, c0:c1] = (_silu(conv) * bv).astype(BF16)
        if f + 1 in down_after:
            k0, k1 = down_after[f + 1]
            down = _dot(act_ref[:, k0:k1], wd_ref[k0:k1, :])
            acc = down if acc is None else acc + down
    acc = acc + xn
    if final:
        acc = _rms(acc, ng_ref[...])
    o_ref[...] = acc


def _out_ffn(x, ya, ybc, wo, fgain, wup, cw, cb, wd, ngain, *, layer, seq, tm, tf, final):
    t = x.shape[0]
    once = functools.partial(_layer_spec, layer, pipeline_mode=pl.Buffered(1))
    return pl.pallas_call(
        functools.partial(_out_ffn_kernel, tm=tm, tf=tf, n_down=N_DOWN, n_pos=seq // tm, final=final),
        grid=(t // tm,),
        in_specs=[
            pl.BlockSpec((tm, D_MODEL), lambda i: (i, 0)),
            pl.BlockSpec((tm, 512), lambda i: (i, 0)),
            pl.BlockSpec((tm, 512), lambda i: (i, 0)),
            once((D_MODEL, D_MODEL)), once((1, D_MODEL)), once((D_MODEL, 2 * D_FF)),
            once((3, D_FF)), once((1, D_FF)), once((D_FF, D_MODEL)),
            pl.BlockSpec((1, D_MODEL), lambda i: (0, 0)),
        ],
        out_specs=pl.BlockSpec((tm, D_MODEL), lambda i: (i, 0)),
        out_shape=jax.ShapeDtypeStruct((t, D_MODEL), F32),
        scratch_shapes=[pltpu.VMEM((tm + 8, D_FF), F32), pltpu.VMEM((tm, D_FF), BF16)],
        compiler_params=pltpu.CompilerParams(dimension_semantics=("arbitrary",), vmem_limit_bytes=VMEM_LIMIT),
        name="out_ffn",
    )(x, ya, ybc, wo, fgain, wup, cw, cb, wd, ngain)


def _rope_tables(seq, dim, base):
    inv = base ** (-(jnp.arange(0, dim, 2, dtype=F32) / dim))
    ang = jnp.arange(seq, dtype=F32)[:, None] * inv[None, :]
    return jnp.cos(ang), jnp.sin(ang)


def _pack_in_weights(w_in, w_uq, w_gate):
    depth, d, _ = w_in.shape
    z = lambda n: jnp.zeros((depth, d, n), w_in.dtype)
    cols = {}
    off = 0
    for name, sz in (("cq", 384), ("ckv", 256), ("kr", 64), ("gq", 128), ("gk", 128), ("gv", 256), ("glr", 16),
                     ("gg", 256), ("rq", 128), ("rk", 128), ("rv", 256), ("rg", 256)):
        cols[name] = w_in[..., off:off + sz]
        off += sz
    w = jnp.concatenate([
        cols["cq"], cols["ckv"], cols["kr"], z(64), cols["gq"], cols["gk"],
        cols["gv"], cols["gg"], cols["rv"], cols["rg"],
        cols["glr"], z(112), cols["rq"], cols["rk"]], axis=-1).astype(BF16)
    uq = w_uq.reshape(depth, MLA_Q_RANK, MLA_HEADS, MLA_NOPE + MLA_ROPE)
    zq = jnp.zeros((depth, MLA_Q_RANK, MLA_HEADS, 64), w_uq.dtype)
    wq = jnp.concatenate([uq, zq], axis=-1).reshape(depth, MLA_Q_RANK, MLA_HEADS * MLA_QK).astype(BF16)
    wg = jnp.zeros((depth, LANE, LANE), w_gate.dtype).at[:, :GLA_GATE_RANK].set(w_gate).astype(BF16)
    return w, wq, wg


def kernel(x, attn_norm, w_in, mla_q_norm, mla_w_uq, mla_kv_norm, mla_w_ukv, mla_out_norm, gla_w_gate, gla_b_gate,
           gla_out_norm, ret_out_norm, w_out, ffn_norm, ffn_w_up, ffn_conv_w, ffn_conv_b, ffn_w_down, final_norm):
    batch, seq, d = x.shape
    depth = w_in.shape[0]
    t = batch * seq
    tm_in, tq, tk, tc, tm_ffn, tf = 512, 1024, 512, 1024, 512, 256

    mcos, msin = _rope_tables(seq, MLA_ROPE, MLA_ROPE_BASE)
    zpad = jnp.zeros((seq, 64), F32)
    cm = jnp.concatenate([mcos, mcos, zpad], axis=1)
    sm = jnp.concatenate([msin, msin, zpad], axis=1)
    rcos, rsin = _rope_tables(seq, RET_DK, RET_ROPE_BASE)
    cr = jnp.tile(jnp.concatenate([rcos, rcos], axis=1), (1, RET_HEADS))
    sr = jnp.tile(jnp.concatenate([rsin, rsin], axis=1), (1, RET_HEADS))

    row = lambda a: a.reshape(depth, 1, -1)
    w, wq, wg = _pack_in_weights(w_in, mla_w_uq, gla_w_gate)
    wkv = mla_w_ukv.astype(BF16)
    attn_g, q_g, kv_g, b_gate = row(attn_norm), row(mla_q_norm), row(mla_kv_norm), row(gla_b_gate)
    mla_g = mla_out_norm.reshape(depth, MLA_HEADS, 1, MLA_V)
    gla_g, ret_g, ffn_g, conv_b = row(gla_out_norm), row(ret_out_norm), row(ffn_norm), row(ffn_conv_b)
    wo, wup, wd = w_out.astype(BF16), ffn_w_up.astype(BF16), ffn_w_down.astype(BF16)

    xf = x.reshape(t, d)
    for l in range(depth):
        q, k, v, lin, la = _in_proj(xf, attn_g, w, q_g, wq, kv_g, wkv, wg, b_gate, cm, sm, cr, sr,
                                    layer=l, seq=seq, tm=tm_in)
        ya = _mla(q, k, v, mla_g, layer=l, batch=batch, seq=seq, tq=tq, tk=tk)
        ybc = _mixers(lin, la, gla_g, ret_g, layer=l, batch=batch, seq=seq, tc=tc)
        xf = _out_ffn(xf, ya, ybc, wo, ffn_g, wup, ffn_conv_w, conv_b, wd, final_norm.reshape(1, d),
                      layer=l, seq=seq, tm=tm_ffn, tf=tf, final=(l == depth - 1))
    return xf.reshape(batch, seq, d)
```

```python
import functools

import jax
import jax.numpy as jnp
from jax import lax
from jax.experimental import pallas as pl
from jax.experimental.pallas import tpu as pltpu

F32 = jnp.float32
BF16 = jnp.bfloat16

D_MODEL = 1024
EPS = 1e-6
MLA_HEADS, MLA_NOPE, MLA_ROPE, MLA_V = 4, 128, 64, 128
MLA_Q_RANK, MLA_KV_RANK = 384, 256
MLA_ROPE_BASE = 10000.0
MLA_QK = 256
GLA_HEADS, GLA_DK, GLA_DV = 4, 32, 64
GLA_GATE_RANK, GLA_GATE_NORM = 16, 16.0
RET_HEADS, RET_DK, RET_DV = 4, 32, 64
RET_ROPE_BASE = 10000.0
CHUNK = 64
D_FF = 2816
LANE = 128
LOG2E = 1.4426950408889634
N_DOWN = 2

W_EDGES = (0, 1024, 2048, 2432)
W_IN = W_EDGES[-1]
LIN_W = 1536

VMEM_LIMIT = 56 * 1024 * 1024


def _dot(a, b):
    return jnp.dot(a, b, preferred_element_type=F32)


def _dot_nt(a, b):
    return lax.dot_general(a, b, (((1,), (1,)), ((), ())), preferred_element_type=F32)


def _dot_tn(a, b):
    return lax.dot_general(a, b, (((0,), (0,)), ((), ())), preferred_element_type=F32)


def _rms(x, gain):
    return x * lax.rsqrt(jnp.mean(x * x, axis=-1, keepdims=True) + EPS) * gain


def _silu(x):
    h = 0.5 * x
    return h + h * jnp.tanh(h)


def _rot_half(x, half):
    lane = lax.broadcasted_iota(jnp.int32, x.shape, 1)
    first = (lane & (2 * half - 1)) < half
    return jnp.where(first, -pltpu.roll(x, LANE - half, axis=1), pltpu.roll(x, half, axis=1))


def _layer_spec(layer, shape, **kw):
    return pl.BlockSpec((None,) + shape, lambda *_: (layer,) + (0,) * len(shape), **kw)


def _in_proj_kernel(x_ref, g_ref, w_ref, qn_ref, wq_ref, kvn_ref, wkv_ref, wg_ref, bg_ref,
                    cm_ref, sm_ref, cr_ref, sr_ref, q_out, k_out, v_out, lin_out, la_out):
    h = _rms(x_ref[...], g_ref[...]).astype(BF16)
    p0, p1, p2 = [_dot(h, w_ref[:, a:b]) for a, b in zip(W_EDGES[:-1], W_EDGES[1:])]

    cq = _rms(p0[:, 0:384], qn_ref[...]).astype(BF16)
    ckv = _rms(p0[:, 384:640], kvn_ref[...]).astype(BF16)
    cm, sm = cm_ref[...], sm_ref[...]
    kr = p0[:, 640:768]
    kr = (kr * cm + _rot_half(kr, MLA_ROPE // 2) * sm).astype(BF16)
    scale = (MLA_NOPE + MLA_ROPE) ** -0.5 * LOG2E
    ones = jnp.ones((x_ref.shape[0], MLA_V), BF16)
    q_all = _dot(cq, wq_ref[...])
    kv_all = _dot(ckv, wkv_ref[...])
    for hd in range(MLA_HEADS):
        o = hd * MLA_QK
        qr = q_all[:, o + 128:o + 256]
        q_out[hd, :, 0:128] = (q_all[:, o:o + 128] * scale).astype(BF16)
        q_out[hd, :, 128:256] = ((qr * cm + _rot_half(qr, MLA_ROPE // 2) * sm) * scale).astype(BF16)
        k_out[hd, :, 0:128] = kv_all[:, o:o + 128].astype(BF16)
        k_out[hd, :, 128:256] = kr
        v_out[hd, :, 0:128] = kv_all[:, o + 128:o + 256].astype(BF16)
        v_out[hd, :, 128:256] = ones

    lin_out[:, 0:128] = (p0[:, 768:896] * (GLA_DK ** -0.5)).astype(BF16)
    lin_out[:, 128:256] = p0[:, 896:1024].astype(BF16)
    lin_out[:, 256:768] = p1[:, 0:512].astype(BF16)
    gate = _dot(p2[:, 0:128].astype(BF16), wg_ref[...]) + bg_ref[...]
    log_sig = jnp.minimum(gate, 0.0) - jnp.log(1.0 + jnp.exp(-jnp.abs(gate)))
    la_out[...] = log_sig / GLA_GATE_NORM

    cr, sr = cr_ref[...], sr_ref[...]
    rq, rk = p2[:, 128:256], p2[:, 256:384]
    lin_out[:, 768:896] = (rq * cr + _rot_half(rq, RET_DK // 2) * sr).astype(BF16)
    lin_out[:, 896:1024] = ((rk * cr + _rot_half(rk, RET_DK // 2) * sr) * (RET_DK ** -0.5)).astype(BF16)
    lin_out[:, 1024:1536] = p1[:, 512:1024].astype(BF16)


def _in_proj(x, gain, w, qn, wq, kvn, wkv, wg, bg, cm, sm, cr, sr, *, layer, seq, tm):
    t = x.shape[0]
    n_pos = seq // tm
    full = functools.partial(_layer_spec, layer)
    tab = pl.BlockSpec((tm, LANE), lambda i: (i % n_pos, 0))
    return pl.pallas_call(
        _in_proj_kernel,
        grid=(t // tm,),
        in_specs=[
            pl.BlockSpec((tm, D_MODEL), lambda i: (i, 0)),
            full((1, D_MODEL)), full((D_MODEL, W_IN)),
            full((1, MLA_Q_RANK)), full((MLA_Q_RANK, MLA_HEADS * MLA_QK)),
            full((1, MLA_KV_RANK)), full((MLA_KV_RANK, MLA_HEADS * 256)),
            full((LANE, LANE)), full((1, LANE)),
            tab, tab, tab, tab,
        ],
        out_specs=[
            pl.BlockSpec((MLA_HEADS, tm, MLA_QK), lambda i: (0, i, 0)),
            pl.BlockSpec((MLA_HEADS, tm, MLA_QK), lambda i: (0, i, 0)),
            pl.BlockSpec((MLA_HEADS, tm, 2 * MLA_V), lambda i: (0, i, 0)),
            pl.BlockSpec((tm, LIN_W), lambda i: (i, 0)),
            pl.BlockSpec((tm, LANE), lambda i: (i, 0)),
        ],
        out_shape=[
            jax.ShapeDtypeStruct((MLA_HEADS, t, MLA_QK), BF16),
            jax.ShapeDtypeStruct((MLA_HEADS, t, MLA_QK), BF16),
            jax.ShapeDtypeStruct((MLA_HEADS, t, 2 * MLA_V), BF16),
            jax.ShapeDtypeStruct((t, LIN_W), BF16),
            jax.ShapeDtypeStruct((t, LANE), F32),
        ],
        compiler_params=pltpu.CompilerParams(
            dimension_semantics=("arbitrary",), vmem_limit_bytes=VMEM_LIMIT),
        name="in_proj",
    )(x, gain, w, qn, wq, kvn, wkv, wg, bg, cm, sm, cr, sr)


def _mla_kernel(q_ref, k_ref, v_ref, g_ref, o_ref, m_ref, acc_ref, s_ref, *, tq, tk, nq):
    def scores(qi, j, r0, nr, buf):
        start = pl.multiple_of(j * tk, tk)
        q = q_ref[0, qi * tq + r0:qi * tq + r0 + nr, :]
        s_ref[2 * (qi & 1) + buf, r0:r0 + nr, :] = _dot_nt(q, k_ref[0, pl.ds(start, tk), :])

    def update(qi, j, r0, nr, buf, diag_col0=None):
        par = qi & 1
        start = pl.multiple_of(j * tk, tk)
        s = s_ref[2 * par + buf, r0:r0 + nr, :]
        if diag_col0 is not None:
            row = lax.broadcasted_iota(jnp.int32, s.shape, 0) + r0
            col = lax.broadcasted_iota(jnp.int32, s.shape, 1) + diag_col0
            s = jnp.where(col <= row, s, -1e30)
        m_old = m_ref[par, r0:r0 + nr, :]
        m_new = jnp.maximum(m_old, jnp.max(s, axis=-1, keepdims=True))
        alpha = jnp.exp2(m_old - m_new)
        p = jnp.exp2(s - jnp.concatenate([m_new] * (tk // LANE), axis=1))
        acc_ref[par, r0:r0 + nr, :] = (jnp.concatenate([alpha, alpha], axis=1) * acc_ref[par, r0:r0 + nr, :]
                                       + _dot(p.astype(BF16), v_ref[0, pl.ds(start, tk), :]))
        m_ref[par, r0:r0 + nr, :] = m_new

    for qi in range(nq):
        par = qi & 1
        m_ref[par] = jnp.full(m_ref.shape[1:], -1e30, F32)
        acc_ref[par] = jnp.zeros(acc_ref.shape[1:], F32)
        scores(qi, 0, 0, tq, 0)

        def pair(j, qi=qi):
            scores(qi, j + 1, 0, tq, 1)
            update(qi, j, 0, tq, 0)
            scores(qi, j + 2, 0, tq, 0)
            update(qi, j + 1, 0, tq, 1)

        def body(jj, carry):
            pair(4 * jj)
            pair(4 * jj + 2)
            return carry

        if qi >= 2:
            lax.fori_loop(0, qi // 2, body, 0)
        if qi % 2:
            pair(2 * (qi - 1))
        scores(qi, 2 * qi + 1, tk, tk, 1)
        update(qi, 2 * qi, 0, tq, 0, diag_col0=0)
        update(qi, 2 * qi + 1, tk, tk, 1, diag_col0=tk)
        acc = acc_ref[par]
        o = acc[:, 0:MLA_V] / acc[:, MLA_V:2 * MLA_V]
        o_ref[qi * tq:(qi + 1) * tq, :] = _rms(o, g_ref[0]).astype(o_ref.dtype)


def _mla(q, k, v, gain, *, layer, batch, seq, tq, tk):
    assert tq == 2 * tk
    t = batch * seq
    return pl.pallas_call(
        functools.partial(_mla_kernel, tq=tq, tk=tk, nq=seq // tq),
        grid=(MLA_HEADS, batch),
        in_specs=[
            pl.BlockSpec((1, seq, MLA_QK), lambda h, b: (h, b, 0)),
            pl.BlockSpec((1, seq, MLA_QK), lambda h, b: (h, b, 0)),
            pl.BlockSpec((1, seq, 2 * MLA_V), lambda h, b: (h, b, 0)),
            pl.BlockSpec((None, 1, 1, MLA_V), lambda h, b: (layer, h, 0, 0)),
        ],
        out_specs=pl.BlockSpec((seq, MLA_V), lambda h, b: (b, h)),
        out_shape=jax.ShapeDtypeStruct((t, MLA_HEADS * MLA_V), BF16),
        scratch_shapes=[pltpu.VMEM((2, tq, LANE), F32), pltpu.VMEM((2, tq, 2 * MLA_V), F32),
                        pltpu.VMEM((4, tq, tk), F32)],
        compiler_params=pltpu.CompilerParams(
            dimension_semantics=("arbitrary", "arbitrary"), vmem_limit_bytes=VMEM_LIMIT),
        name="mla",
    )(q, k, v, gain)


def _mm_split2(sel, x):
    hi = x.astype(BF16)
    lo = (x - hi.astype(F32)).astype(BF16)
    return _dot(sel, hi) + _dot(sel, lo)


def _head_rows(a, n_lanes_per_head):
    lane = lax.broadcasted_iota(jnp.int32, a.shape, 1)
    shift = n_lanes_per_head.bit_length() - 1
    zero = jnp.zeros_like(a)
    return jnp.concatenate(
        [jnp.where((lane >> shift) == hd, a, zero) for hd in range(4)], axis=0)


def _mixers_kernel(lin_ref, la_ref, gng_ref, rng_ref, y_ref, sg_ref, sr_ref, *, nchunk):
    C = CHUNK

    @pl.when(pl.program_id(1) == 0)
    def _():
        sg_ref[...] = jnp.zeros(sg_ref.shape, F32)
        sr_ref[...] = jnp.zeros(sr_ref.shape, F32)

    def rows(c):
        return slice(c * C, (c + 1) * C)

    def lanes(c):
        return slice(c * LANE, (c + 1) * LANE)

    def wide(a):
        return jnp.concatenate([a[rows(c)] for c in range(nchunk)], axis=1)

    r_cc = lax.broadcasted_iota(jnp.int32, (C, C), 0)
    c_cc = lax.broadcasted_iota(jnp.int32, (C, C), 1)
    row_w = lax.broadcasted_iota(jnp.int32, (C, nchunk * LANE), 0)
    i_a = lax.broadcasted_iota(jnp.int32, (C, 256), 0)
    j_a = lax.broadcasted_iota(jnp.int32, (C, 256), 1) & (C - 1)
    h_a = lax.broadcasted_iota(jnp.int32, (C, 256), 1) >> 6
    sm = (lax.broadcasted_iota(jnp.int32, (256, LANE), 0) >> 6) == (
        lax.broadcasted_iota(jnp.int32, (256, LANE), 1) >> 5)
    r256 = lax.broadcasted_iota(jnp.int32, (256, 256), 0)
    c256 = lax.broadcasted_iota(jnp.int32, (256, 256), 1)
    ind = jnp.where((r256 >> 6) == (c256 >> 6), 1.0 / 64, 0.0).astype(BF16)

    q_w = wide(lin_ref[:, 0:128].astype(F32))
    k_w = wide(lin_ref[:, 128:256].astype(F32))
    g_w = wide(la_ref[...])
    tri = jnp.where(c_cc <= r_cc, 1.0, 0.0).astype(BF16)
    b_w = _mm_split2(tri, g_w)
    b_end = b_w[C - 1:C, :]

    a_cat = [None] * nchunk

    def put_level(ql, kl, keep):
        ql, kl = ql.astype(BF16), kl.astype(BF16)
        for c in range(nchunk):
            p = _dot_nt(ql[:, lanes(c)], _head_rows(kl[:, lanes(c)], GLA_DK))
            a_cat[c] = p if a_cat[c] is None else jnp.where(keep, p, a_cat[c])

    def boundary(s):
        g = 2 * s
        if g >= 8:
            return jnp.concatenate(
                [jnp.broadcast_to(b_w[m * g + s - 1:m * g + s, :], (g, b_w.shape[1])) for m in range(C // g)], axis=0)
        pos = row_w & (g - 1)
        out = b_w
        for o in range(g):
            if o != s - 1:
                out = jnp.where(pos == o, pltpu.roll(b_w, (o - (s - 1)) % C, axis=0), out)
        return out

    for s in (32, 16, 8, 4, 2, 1):
        sh = s.bit_length() - 1
        odd = ((row_w >> sh) & 1) == 1
        e = jnp.exp(-jnp.abs(b_w - boundary(s)))
        put_level(jnp.where(odd, q_w * e, 0.0), jnp.where(odd, 0.0, k_w * e), (i_a >> (sh + 1)) == (j_a >> (sh + 1)))
    put_level(q_w, k_w, i_a == j_a)

    qd_w = (q_w * jnp.exp(b_w)).astype(BF16)
    kd_w = (k_w * jnp.exp(b_end - b_w)).astype(BF16)
    dec_w = jnp.exp(b_end)
    state = sg_ref[...]
    outs = []
    for c in range(nchunk):
        vc = lin_ref[rows(c), 256:512]
        intra = _dot(a_cat[c].astype(BF16), _head_rows(vc, GLA_DV))
        inter = _dot_nt(qd_w[:, lanes(c)], state.astype(BF16))
        upd = _dot_tn(vc, kd_w[:, lanes(c)])
        state = state * dec_w[:, lanes(c)] + jnp.where(sm, upd, 0.0)
        outs.append(inter + intra)
    sg_ref[...] = state
    o = jnp.concatenate(outs, axis=0)
    o = o * lax.rsqrt(_dot((o * o).astype(BF16), ind) + EPS) * gng_ref[...]
    y_ref[:, 0:256] = (o * _silu(lin_ref[:, 512:768].astype(F32))).astype(y_ref.dtype)

    def log_gamma(hd):
        v = jnp.where(hd == 0, 2.0 ** -5, jnp.where(hd == 1, 2.0 ** -6, jnp.where(hd == 2, 2.0 ** -7, 2.0 ** -8)))
        return jnp.log(1.0 - v)

    lg_a = log_gamma(h_a)
    diff = (i_a - j_a).astype(F32)
    d_cat = jnp.where(diff >= 0, jnp.exp(lg_a * jnp.maximum(diff, 0.0)), 0.0)
    row_c = lax.broadcasted_iota(jnp.int32, (C, LANE), 0).astype(F32)
    lg_c = log_gamma(lax.broadcasted_iota(jnp.int32, (C, LANE), 1) >> 5)
    k_dec = jnp.exp(lg_c * (C - 1.0 - row_c))
    q_dec = jnp.exp(lg_c * (row_c + 1.0))
    c_dec = jnp.exp(lg_c[0:1, :] * float(C))
    state = sr_ref[...]
    outs = []
    for c in range(nchunk):
        qc = lin_ref[rows(c), 768:896]
        kc = lin_ref[rows(c), 896:1024]
        vc = lin_ref[rows(c), 1024:1280]
        p = _dot_nt(qc, _head_rows(kc, RET_DK)) * d_cat
        intra = _dot(p.astype(BF16), _head_rows(vc, RET_DV))
        inter = _dot_nt((qc.astype(F32) * q_dec).astype(BF16), state.astype(BF16))
        upd = _dot_tn(vc, (kc.astype(F32) * k_dec).astype(BF16))
        state = state * c_dec + jnp.where(sm, upd, 0.0)
        outs.append(inter + intra)
    sr_ref[...] = state
    o = jnp.concatenate(outs, axis=0)
    o = o * lax.rsqrt(_dot((o * o).astype(BF16), ind) + EPS) * rng_ref[...]
    y_ref[:, 256:512] = (o * _silu(lin_ref[:, 1280:1536].astype(F32))).astype(y_ref.dtype)


def _mixers(lin, la, gla_gain, ret_gain, *, layer, batch, seq, tc):
    n = seq // tc
    t = batch * seq
    return pl.pallas_call(
        functools.partial(_mixers_kernel, nchunk=tc // CHUNK),
        grid=(batch, n),
        in_specs=[
            pl.BlockSpec((tc, LIN_W), lambda b, c: (b * n + c, 0)),
            pl.BlockSpec((tc, LANE), lambda b, c: (b * n + c, 0)),
            _layer_spec(layer, (1, 256)),
            _layer_spec(layer, (1, 256)),
        ],
        out_specs=pl.BlockSpec((tc, 512), lambda b, c: (b * n + c, 0)),
        out_shape=jax.ShapeDtypeStruct((t, 512), BF16),
        scratch_shapes=[pltpu.VMEM((256, LANE), F32), pltpu.VMEM((256, LANE), F32)],
        compiler_params=pltpu.CompilerParams(
            dimension_semantics=("arbitrary", "arbitrary"), vmem_limit_bytes=VMEM_LIMIT),
        name="mixers",
    )(lin, la, gla_gain, ret_gain)


def _out_ffn_kernel(x_ref, ya_ref, ybc_ref, wo_ref, fg_ref, wup_ref, cw_ref, cb_ref, wd_ref, ng_ref,
                    o_ref, abuf_ref, act_ref, *, tm, tf, n_down, n_pos, final):
    i = pl.program_id(0)
    xn = x_ref[...] + _dot(ya_ref[...], wo_ref[0:512, :]) + _dot(ybc_ref[...], wo_ref[512:1024, :])
    hn = _rms(xn, fg_ref[...]).astype(BF16)

    @pl.when(i % n_pos == 0)
    def _():
        abuf_ref[0:8, :] = jnp.zeros((8, D_FF), F32)

    @pl.when(i % n_pos != 0)
    def _():
        abuf_ref[0:8, :] = abuf_ref[tm:tm + 8, :]

    nf = D_FF // tf

    def up(f):
        c0, c1 = f * tf, (f + 1) * tf
        a = _dot(hn, wup_ref[:, c0:c1])
        abuf_ref[8:tm + 8, c0:c1] = a
        return a, _dot(hn, wup_ref[:, D_FF + c0:D_FF + c1])

    bounds = [round(g * nf / n_down) for g in range(n_down + 1)]
    down_after = {bounds[g + 1]: (bounds[g] * tf, bounds[g + 1] * tf) for g in range(n_down)}
    acc = None
    ahead = 2
    pending = [up(f) for f in range(ahead)]
    for f in range(nf):
        c0, c1 = f * tf, (f + 1) * tf
        a, bv = pending.pop(0)
        if f + ahead < nf:
            pending.append(up(f + ahead))
        cw = cw_ref[:, c0:c1]
        conv = (cb_ref[:, c0:c1] + cw[0:1, :] * abuf_ref[6:tm + 6, c0:c1]
                + cw[1:2, :] * abuf_ref[7:tm + 7, c0:c1] + cw[2:3, :] * a)
        act_ref[:, c0:c1] = (_silu(conv) * bv).astype(BF16)
        if f + 1 in down_after:
            k0, k1 = down_after[f + 1]
            down = _dot(act_ref[:, k0:k1], wd_ref[k0:k1, :])
            acc = down if acc is None else acc + down
    acc = acc + xn
    if final:
        acc = _rms(acc, ng_ref[...])
    o_ref[...] = acc


def _out_ffn(x, ya, ybc, wo, fgain, wup, cw, cb, wd, ngain, *, layer, seq, tm, tf, final):
    t = x.shape[0]
    once = functools.partial(_layer_spec, layer, pipeline_mode=pl.Buffered(1))
    return pl.pallas_call(
        functools.partial(_out_ffn_kernel, tm=tm, tf=tf, n_down=N_DOWN, n_pos=seq // tm, final=final),
        grid=(t // tm,),
        in_specs=[
            pl.BlockSpec((tm, D_MODEL), lambda i: (i, 0)),
            pl.BlockSpec((tm, 512), lambda i: (i, 0)),
            pl.BlockSpec((tm, 512), lambda i: (i, 0)),
            once((D_MODEL, D_MODEL)), once((1, D_MODEL)), once((D_MODEL, 2 * D_FF)),
            once((3, D_FF)), once((1, D_FF)), once((D_FF, D_MODEL)),
            pl.BlockSpec((1, D_MODEL), lambda i: (0, 0)),
        ],
        out_specs=pl.BlockSpec((tm, D_MODEL), lambda i: (i, 0)),
        out_shape=jax.ShapeDtypeStruct((t, D_MODEL), F32),
        scratch_shapes=[pltpu.VMEM((tm + 8, D_FF), F32), pltpu.VMEM((tm, D_FF), BF16)],
        compiler_params=pltpu.CompilerParams(dimension_semantics=("arbitrary",), vmem_limit_bytes=VMEM_LIMIT),
        name="out_ffn",
    )(x, ya, ybc, wo, fgain, wup, cw, cb, wd, ngain)


def _rope_tables(seq, dim, base):
    inv = base ** (-(jnp.arange(0, dim, 2, dtype=F32) / dim))
    ang = jnp.arange(seq, dtype=F32)[:, None] * inv[None, :]
    return jnp.cos(ang), jnp.sin(ang)


def _pack_in_weights(w_in, w_uq, w_gate):
    depth, d, _ = w_in.shape
    z = lambda n: jnp.zeros((depth, d, n), w_in.dtype)
    cols = {}
    off = 0
    for name, sz in (("cq", 384), ("ckv", 256), ("kr", 64), ("gq", 128), ("gk", 128), ("gv", 256), ("glr", 16),
                     ("gg", 256), ("rq", 128), ("rk", 128), ("rv", 256), ("rg", 256)):
        cols[name] = w_in[..., off:off + sz]
        off += sz
    w = jnp.concatenate([
        cols["cq"], cols["ckv"], cols["kr"], z(64), cols["gq"], cols["gk"],
        cols["gv"], cols["gg"], cols["rv"], cols["rg"],
        cols["glr"], z(112), cols["rq"], cols["rk"]], axis=-1).astype(BF16)
    uq = w_uq.reshape(depth, MLA_Q_RANK, MLA_HEADS, MLA_NOPE + MLA_ROPE)
    zq = jnp.zeros((depth, MLA_Q_RANK, MLA_HEADS, 64), w_uq.dtype)
    wq = jnp.concatenate([uq, zq], axis=-1).reshape(depth, MLA_Q_RANK, MLA_HEADS * MLA_QK).astype(BF16)
    wg = jnp.zeros((depth, LANE, LANE), w_gate.dtype).at[:, :GLA_GATE_RANK].set(w_gate).astype(BF16)
    return w, wq, wg


def kernel(x, attn_norm, w_in, mla_q_norm, mla_w_uq, mla_kv_norm, mla_w_ukv, mla_out_norm, gla_w_gate, gla_b_gate,
           gla_out_norm, ret_out_norm, w_out, ffn_norm, ffn_w_up, ffn_conv_w, ffn_conv_b, ffn_w_down, final_norm):
    batch, seq, d = x.shape
    depth = w_in.shape[0]
    t = batch * seq
    tm_in, tq, tk, tc, tm_ffn, tf = 1024, 1024, 512, 1024, 512, 256

    mcos, msin = _rope_tables(seq, MLA_ROPE, MLA_ROPE_BASE)
    zpad = jnp.zeros((seq, 64), F32)
    cm = jnp.concatenate([mcos, mcos, zpad], axis=1)
    sm = jnp.concatenate([msin, msin, zpad], axis=1)
    rcos, rsin = _rope_tables(seq, RET_DK, RET_ROPE_BASE)
    cr = jnp.tile(jnp.concatenate([rcos, rcos], axis=1), (1, RET_HEADS))
    sr = jnp.tile(jnp.concatenate([rsin, rsin], axis=1), (1, RET_HEADS))

    row = lambda a: a.reshape(depth, 1, -1)
    w, wq, wg = _pack_in_weights(w_in, mla_w_uq, gla_w_gate)
    wkv = mla_w_ukv.astype(BF16)
    attn_g, q_g, kv_g, b_gate = row(attn_norm), row(mla_q_norm), row(mla_kv_norm), row(gla_b_gate)
    mla_g = mla_out_norm.reshape(depth, MLA_HEADS, 1, MLA_V)
    gla_g, ret_g, ffn_g, conv_b = row(gla_out_norm), row(ret_out_norm), row(ffn_norm), row(ffn_conv_b)
    wo, wup, wd = w_out.astype(BF16), ffn_w_up.astype(BF16), ffn_w_down.astype(BF16)

    xf = x.reshape(t, d)
    for l in range(depth):
        q, k, v, lin, la = _in_proj(xf, attn_g, w, q_g, wq, kv_g, wkv, wg, b_gate, cm, sm, cr, sr,
                                    layer=l, seq=seq, tm=tm_in)
        ya = _mla(q, k, v, mla_g, layer=l, batch=batch, seq=seq, tq=tq, tk=tk)
        ybc = _mixers(lin, la, gla_g, ret_g, layer=l, batch=batch, seq=seq, tc=tc)
        xf = _out_ffn(xf, ya, ybc, wo, ffn_g, wup, ffn_conv_w, conv_b, wd, final_norm.reshape(1, d),
                      layer=l, seq=seq, tm=tm_ffn, tf=tf, final=(l == depth - 1))
    return xf.reshape(batch, seq, d)
```
